```python
import math
import jax
import jax.numpy as jnp
from jax import lax
import numpy as np

D_MODEL = 2048
BATCH = 8
SEQ = 2048
DEPTH = 4

HEAD_DIM = 128
A_HEADS = 8
A_WIDTH = A_HEADS * HEAD_DIM
A_BRANCHES = ((128, 1), (512, 4), (2048, 16))
SSD_HEADS = 16
SSD_HEAD_DIM = 64
SSD_WIDTH = SSD_HEADS * SSD_HEAD_DIM
SSD_GROUPS = 4
SSD_STATE = 128
SSD_CONV = 4
SSD_CHUNK = 128
SSD_CONV_WIDTH = SSD_WIDTH + 2 * SSD_GROUPS * SSD_STATE
EVEN_IN_WIDTH = 3 * A_WIDTH + SSD_WIDTH + SSD_CONV_WIDTH + SSD_HEADS
MIX_WIDTH = A_WIDTH + SSD_WIDTH
C_HEADS = 16
C_WIDTH = C_HEADS * HEAD_DIM
MOBA_BLOCK = 256
MOBA_TOPK = 3
MOBA_QUERY_CHUNK = 8
FFN_DIM = 5632
N_EXPERTS = 8
TOP_K = 2
LN_EPS = 1e-5
RMS_EPS = 1e-5
DEEPNORM_ALPHA = (2 * DEPTH) ** 0.25
DEEPNORM_BETA = (8 * DEPTH) ** -0.25
N_EVEN = (DEPTH + 1) // 2
N_ODD = DEPTH // 2
NEG = -1e30

kernel_name = 'hybrid_dilated_ssd_moba_moe'


def layer_norm(x, g, b):
    xf = x.astype(jnp.float32)
    mu = jnp.mean(xf, axis=-1, keepdims=True)
    xc = xf - mu
    var = jnp.mean(xc * xc, axis=-1, keepdims=True)
    return (xc * lax.rsqrt(var + LN_EPS) * g + b).astype(x.dtype)


def softmax_stats(s):
    m = jnp.max(s, axis=-1, keepdims=True)
    p = jnp.exp(s - m)
    den = jnp.sum(p, axis=-1, keepdims=True)
    return p / den, (m + jnp.log(den))[..., 0]


def dilated_branch(q, k, v, window, dilation):
    bsz, seq, nh, hd = q.shape
    band = window // dilation
    sub_len = seq // dilation
    n_blk = -(-sub_len // band)
    pad = n_blk * band - sub_len

    def strided(a):
        return a.reshape(bsz, sub_len, dilation, nh, hd).transpose(0, 2, 1, 3, 4)

    qs = jnp.pad(strided(q), ((0, 0), (0, 0), (0, pad), (0, 0), (0, 0)))
    qs = qs.reshape(bsz, dilation, n_blk, band, nh, hd)

    def band_keys(a):
        a = jnp.pad(strided(a), ((0, 0), (0, 0), (band, pad), (0, 0), (0, 0)))
        a = a.reshape(bsz, dilation, n_blk + 1, band, nh, hd)
        return jnp.concatenate([a[:, :, :-1], a[:, :, 1:]], axis=3)

    kb, vb = band_keys(k), band_keys(v)
    s = jnp.einsum('bdnqhe,bdnkhe->bdnhqk', qs, kb).astype(jnp.float32) * (hd ** -0.5)
    qi = jnp.arange(band)[:, None]
    kj = jnp.arange(2 * band)[None, :]
    dist = band + qi - kj
    in_win = (dist >= 0) & (dist <= band)
    key_pos = (jnp.arange(n_blk)[:, None, None] - 1) * band + kj[None]
    mask = in_win[None] & (key_pos >= 0)
    s = jnp.where(mask[None, None, :, None], s, NEG)
    p, lse = softmax_stats(s)
    o = jnp.einsum('bdnhqk,bdnkhe->bdnqhe', p.astype(v.dtype), vb)
    o = o.reshape(bsz, dilation, n_blk * band, nh, hd)[:, :, :sub_len]
    o = o.transpose(0, 2, 1, 3, 4).reshape(bsz, seq, nh, hd)
    lse = lse.transpose(0, 1, 2, 4, 3).reshape(bsz, dilation, n_blk * band, nh)[:, :, :sub_len]
    lse = lse.transpose(0, 2, 1, 3).reshape(bsz, seq, nh)
    return o, lse


def dilated_attention(q, k, v):
    outs, lses = [], []
    for window, dilation in A_BRANCHES:
        o, l = dilated_branch(q, k, v, window, dilation)
        outs.append(o)
        lses.append(l)
    w = jax.nn.softmax(jnp.stack(lses), axis=0)
    o = jnp.einsum('ibsh,ibshe->bshe', w, jnp.stack(outs).astype(jnp.float32))
    bsz, seq = q.shape[:2]
    return o.astype(q.dtype).reshape(bsz, seq, A_WIDTH)


def causal_dwconv(x, w, b):
    ksz, ch = w.shape
    y = lax.conv_general_dilated(x, w[:, None, :], window_strides=(1,),
                                 padding=((ksz - 1, 0),),
                                 dimension_numbers=('NWC', 'WIO', 'NWC'),
                                 feature_group_count=ch)
    return y + b


def ssd_scan(x, dt, a_head, bmat, cmat):
    bsz, seq, nh, hp = x.shape
    ng, ns = bmat.shape[2], bmat.shape[3]
    rep = nh // ng
    nc = seq // SSD_CHUNK
    q = SSD_CHUNK
    xdt = (x * dt[..., None]).reshape(bsz, nc, q, ng, rep, hp)
    a_cs = jnp.cumsum((dt * a_head).reshape(bsz, nc, q, nh), axis=2)
    bc = bmat.reshape(bsz, nc, q, ng, ns)
    cc = cmat.reshape(bsz, nc, q, ng, ns)
    causal = jnp.tril(jnp.ones((q, q), dtype=bool))
    seg = a_cs[:, :, :, None, :] - a_cs[:, :, None, :, :]
    decay_in = jnp.exp(jnp.where(causal[None, None, :, :, None], seg, -jnp.inf))
    decay_in = decay_in.reshape(bsz, nc, q, q, ng, rep)
    cb = jnp.einsum('bclgn,bcsgn->bclsg', cc, bc)
    y_diag = jnp.einsum('bclsgr,bcsgrp->bclgrp', cb[..., None] * decay_in, xdt)
    decay_st = jnp.exp(a_cs[:, :, -1:, :] - a_cs).reshape(bsz, nc, q, ng, rep)
    states = jnp.einsum('bclgn,bclgr,bclgrp->bcgrpn', bc, decay_st, xdt)
    chunk_decay = jnp.exp(a_cs[:, :, -1, :]).reshape(bsz, nc, ng, rep)

    def step(h, inp):
        st, dec = inp
        return dec[..., None, None] * h + st, h

    h0 = jnp.zeros((bsz, ng, rep, hp, ns), jnp.float32)
    _, prev = lax.scan(step, h0, (jnp.moveaxis(states.astype(jnp.float32), 1, 0),
                                  jnp.moveaxis(chunk_decay, 1, 0)))
    prev = jnp.moveaxis(prev, 0, 1)
    y_off = jnp.einsum('bclgn,bcgrpn,bclgr->bclgrp', cc, prev,
                       jnp.exp(a_cs).reshape(bsz, nc, q, ng, rep))
    return (y_diag + y_off).reshape(bsz, seq, nh, hp)


def ssd_mixer(z, xbc, dt_raw, conv_w, conv_b, dt_bias, a_log, d_skip, norm_w):
    bsz, seq = z.shape[:2]
    xbc = jax.nn.silu(causal_dwconv(xbc, conv_w, conv_b))
    xs, bm, cm = jnp.split(xbc, [SSD_WIDTH, SSD_WIDTH + SSD_GROUPS * SSD_STATE], axis=-1)
    xs = xs.reshape(bsz, seq, SSD_HEADS, SSD_HEAD_DIM)
    bm = bm.reshape(bsz, seq, SSD_GROUPS, SSD_STATE)
    cm = cm.reshape(bsz, seq, SSD_GROUPS, SSD_STATE)
    dt = jax.nn.softplus(dt_raw.astype(jnp.float32) + dt_bias.astype(jnp.float32))
    a_head = -jnp.exp(a_log.astype(jnp.float32))
    y = ssd_scan(xs, dt, a_head, bm, cm) + d_skip[:, None] * xs
    y = y.reshape(bsz, seq, SSD_WIDTH) * jax.nn.silu(z)
    yg = y.astype(jnp.float32).reshape(bsz, seq, SSD_GROUPS, SSD_WIDTH // SSD_GROUPS)
    yg = yg * lax.rsqrt(jnp.mean(yg * yg, axis=-1, keepdims=True) + RMS_EPS)
    return (yg.reshape(bsz, seq, SSD_WIDTH) * norm_w).astype(z.dtype)


def even_mixer(x, w_in, conv_w, conv_b, dt_bias, a_log, d_skip, ssd_norm, w_out):
    bsz, seq = x.shape[:2]
    proj = x @ w_in
    cuts = [A_WIDTH, 2 * A_WIDTH, 3 * A_WIDTH, 3 * A_WIDTH + SSD_WIDTH,
            3 * A_WIDTH + SSD_WIDTH + SSD_CONV_WIDTH]
    q, k, v, z, xbc, dt_raw = jnp.split(proj, cuts, axis=-1)
    heads = lambda a: a.reshape(bsz, seq, A_HEADS, HEAD_DIM)
    y_a = dilated_attention(heads(q), heads(k), heads(v))
    y_b = ssd_mixer(z, xbc, dt_raw, conv_w, conv_b, dt_bias, a_log, d_skip, ssd_norm)
    return jnp.concatenate([y_a, y_b.astype(y_a.dtype)], axis=-1) @ w_out


def moba_attention(q, k, v):
    bsz, seq, nh, hd = q.shape
    nb = -(-seq // MOBA_BLOCK)
    sp = nb * MOBA_BLOCK
    pad = sp - seq
    scale = hd ** -0.5

    def blocks(a):
        a = jnp.pad(a, ((0, 0), (0, pad), (0, 0), (0, 0)))
        return a.reshape(bsz, nb, MOBA_BLOCK, nh, hd).transpose(0, 3, 1, 2, 4)

    qb, kb, vb = blocks(q), blocks(k), blocks(v)
    causal = jnp.tril(jnp.ones((MOBA_BLOCK, MOBA_BLOCK), dtype=bool))
    s = jnp.einsum('bhnqe,bhnke->bhnqk', qb, kb).astype(jnp.float32) * scale
    p, lse_self = softmax_stats(jnp.where(causal, s, NEG))
    o_self = jnp.einsum('bhnqk,bhnke->bhnqe', p.astype(v.dtype), vb).reshape(bsz, nh, sp, hd)
    lse_self = lse_self.reshape(bsz, nh, sp)
    qf = qb.reshape(bsz, nh, sp, hd)
    k_mean = jnp.mean(kb.astype(jnp.float32), axis=3)
    gate = jnp.einsum('bhse,bhne->bhsn', qf.astype(jnp.float32), k_mean)
    q_blk = jnp.arange(sp) // MOBA_BLOCK
    past = jnp.arange(nb)[None, :] < q_blk[:, None]
    gate = jnp.where(past, gate, NEG)
    n_sel = min(MOBA_TOPK, nb)
    _, sel = lax.top_k(gate, n_sel)
    sel_valid = jnp.arange(n_sel)[None, :] < q_blk[:, None]
    nq = sp // MOBA_QUERY_CHUNK
    qc_all = jnp.moveaxis(qf.reshape(bsz, nh, nq, MOBA_QUERY_CHUNK, hd), 2, 0)
    ic_all = jnp.moveaxis(sel.reshape(bsz, nh, nq, MOBA_QUERY_CHUNK, n_sel), 2, 0)
    vc_all = sel_valid.reshape(nq, MOBA_QUERY_CHUNK, n_sel)
    gather_blocks = jax.vmap(jax.vmap(lambda blk, idx: blk[idx]))

    def past_part(args):
        qc, ic, vc = args
        kg = gather_blocks(kb, ic)
        vg = gather_blocks(vb, ic)
        s = jnp.einsum('bhqe,bhqrle->bhqrl', qc, kg).astype(jnp.float32) * scale
        s = jnp.where(vc[None, None, :, :, None], s, NEG)
        s = s.reshape(bsz, nh, MOBA_QUERY_CHUNK, n_sel * MOBA_BLOCK)
        p, lse = softmax_stats(s)
        vg = vg.reshape(bsz, nh, MOBA_QUERY_CHUNK, n_sel * MOBA_BLOCK, hd)
        return jnp.einsum('bhqj,bhqje->bhqe', p.astype(v.dtype), vg), lse

    o_past, lse_past = lax.map(past_part, (qc_all, ic_all, vc_all))
    o_past = jnp.moveaxis(o_past, 0, 2).reshape(bsz, nh, sp, hd)
    lse_past = jnp.moveaxis(lse_past, 0, 2).reshape(bsz, nh, sp)
    w_self = jax.nn.sigmoid(lse_self - lse_past)[..., None]
    o = w_self * o_self.astype(jnp.float32) + (1.0 - w_self) * o_past.astype(jnp.float32)
    o = o[:, :, :seq].transpose(0, 2, 1, 3).reshape(bsz, seq, nh * hd)
    return o.astype(q.dtype)


def odd_mixer(x, w_qkv, w_out):
    bsz, seq = x.shape[:2]
    q, k, v = jnp.split(x @ w_qkv, 3, axis=-1)
    heads = lambda a: a.reshape(bsz, seq, C_HEADS, HEAD_DIM)
    return moba_attention(heads(q), heads(k), heads(v)) @ w_out


def swiglu(x, w_gu, w_down):
    g, u = jnp.split(x @ w_gu, 2, axis=-1)
    return (jax.nn.silu(g) * u) @ w_down


def moe_swiglu(x, w_router, b_router, w_gu, w_down):
    bsz, seq, d = x.shape
    xt = x.reshape(bsz * seq, d)
    logits = xt.astype(jnp.float32) @ w_router.astype(jnp.float32) + b_router.astype(jnp.float32)
    top_val, top_idx = lax.top_k(logits, TOP_K)
    gates = jax.nn.softmax(top_val, axis=-1)
    gate_full = jnp.sum(jax.nn.one_hot(top_idx, N_EXPERTS, dtype=jnp.float32) * gates[..., None], axis=1)
    gate_full = gate_full.astype(x.dtype)
    out = gate_full[:, 0:1] * swiglu(xt, w_gu[0], w_down[0])
    for e in range(1, N_EXPERTS):
        out = out + gate_full[:, e:e + 1] * swiglu(xt, w_gu[e], w_down[e])
    return out.reshape(bsz, seq, d)


def setup_inputs(seed: int = 0) -> dict:
    key = jax.random.key(seed)
    ks = jax.random.split(key, 32)
    f32 = jnp.float32
    nrm = lambda k, shape, sc: jax.random.normal(k, shape, f32) * sc
    x = nrm(ks[0], (BATCH, SEQ, D_MODEL), 1.0)
    ev_w_in = nrm(ks[1], (N_EVEN, D_MODEL, EVEN_IN_WIDTH), D_MODEL ** -0.5)
    ev_conv_w = nrm(ks[2], (N_EVEN, SSD_CONV, SSD_CONV_WIDTH), SSD_CONV ** -0.5)
    ev_conv_b = nrm(ks[3], (N_EVEN, SSD_CONV_WIDTH), 0.02)
    dt0 = jnp.exp(jax.random.uniform(ks[4], (N_EVEN, SSD_HEADS), f32, math.log(1e-3), math.log(1e-1)))
    ev_dt_bias = dt0 + jnp.log(-jnp.expm1(-dt0))
    ev_a_log = jnp.log(jax.random.uniform(ks[5], (N_EVEN, SSD_HEADS), f32, 1.0, 16.0))
    ev_d_skip = 1.0 + nrm(ks[6], (N_EVEN, SSD_HEADS), 0.1)
    ev_ssd_norm = 1.0 + nrm(ks[7], (N_EVEN, SSD_WIDTH), 0.02)
    ev_w_out = nrm(ks[8], (N_EVEN, MIX_WIDTH, D_MODEL), MIX_WIDTH ** -0.5 * DEEPNORM_BETA)
    ev_ln1_g = 1.0 + nrm(ks[9], (N_EVEN, D_MODEL), 0.02)
    ev_ln1_b = nrm(ks[10], (N_EVEN, D_MODEL), 0.02)
    ev_ffn_w_gu = nrm(ks[11], (N_EVEN, D_MODEL, 2 * FFN_DIM), D_MODEL ** -0.5)
    ev_ffn_w_down = nrm(ks[12], (N_EVEN, FFN_DIM, D_MODEL), FFN_DIM ** -0.5 * DEEPNORM_BETA)
    ev_ln2_g = 1.0 + nrm(ks[13], (N_EVEN, D_MODEL), 0.02)
    ev_ln2_b = nrm(ks[14], (N_EVEN, D_MODEL), 0.02)
    od_w_qkv = nrm(ks[15], (N_ODD, D_MODEL, 3 * C_WIDTH), D_MODEL ** -0.5)
    od_w_out = nrm(ks[16], (N_ODD, C_WIDTH, D_MODEL), C_WIDTH ** -0.5 * DEEPNORM_BETA)
    od_ln1_g = 1.0 + nrm(ks[17], (N_ODD, D_MODEL), 0.02)
    od_ln1_b = nrm(ks[18], (N_ODD, D_MODEL), 0.02)
    od_router_w = nrm(ks[19], (N_ODD, D_MODEL, N_EXPERTS), D_MODEL ** -0.5)
    od_router_b = nrm(ks[20], (N_ODD, N_EXPERTS), 0.01)
    od_exp_w_gu = nrm(ks[21], (N_ODD, N_EXPERTS, D_MODEL, 2 * FFN_DIM), D_MODEL ** -0.5)
    od_exp_w_down = nrm(ks[22], (N_ODD, N_EXPERTS, FFN_DIM, D_MODEL), FFN_DIM ** -0.5 * DEEPNORM_BETA)
    od_ln2_g = 1.0 + nrm(ks[23], (N_ODD, D_MODEL), 0.02)
    od_ln2_b = nrm(ks[24], (N_ODD, D_MODEL), 0.02)
    return {'x': x,
            'ev_w_in': ev_w_in, 'ev_conv_w': ev_conv_w, 'ev_conv_b': ev_conv_b,
            'ev_dt_bias': ev_dt_bias, 'ev_a_log': ev_a_log, 'ev_d_skip': ev_d_skip,
            'ev_ssd_norm': ev_ssd_norm, 'ev_w_out': ev_w_out,
            'ev_ln1_g': ev_ln1_g, 'ev_ln1_b': ev_ln1_b,
            'ev_ffn_w_gu': ev_ffn_w_gu, 'ev_ffn_w_down': ev_ffn_w_down,
            'ev_ln2_g': ev_ln2_g, 'ev_ln2_b': ev_ln2_b,
            'od_w_qkv': od_w_qkv, 'od_w_out': od_w_out,
            'od_ln1_g': od_ln1_g, 'od_ln1_b': od_ln1_b,
            'od_router_w': od_router_w, 'od_router_b': od_router_b,
            'od_exp_w_gu': od_exp_w_gu, 'od_exp_w_down': od_exp_w_down,
            'od_ln2_g': od_ln2_g, 'od_ln2_b': od_ln2_b}


def reference(x, ev_w_in, ev_conv_w, ev_conv_b, ev_dt_bias, ev_a_log, ev_d_skip,
              ev_ssd_norm, ev_w_out, ev_ln1_g, ev_ln1_b, ev_ffn_w_gu, ev_ffn_w_down,
              ev_ln2_g, ev_ln2_b, od_w_qkv, od_w_out, od_ln1_g, od_ln1_b,
              od_router_w, od_router_b, od_exp_w_gu, od_exp_w_down, od_ln2_g, od_ln2_b):
    for layer in range(DEPTH):
        i = layer // 2
        if layer % 2 == 0:
            h = even_mixer(x, ev_w_in[i], ev_conv_w[i], ev_conv_b[i], ev_dt_bias[i],
                           ev_a_log[i], ev_d_skip[i], ev_ssd_norm[i], ev_w_out[i])
            x = layer_norm(DEEPNORM_ALPHA * x + h, ev_ln1_g[i], ev_ln1_b[i])
            h = swiglu(x, ev_ffn_w_gu[i], ev_ffn_w_down[i])
            x = layer_norm(DEEPNORM_ALPHA * x + h, ev_ln2_g[i], ev_ln2_b[i])
        else:
            h = odd_mixer(x, od_w_qkv[i], od_w_out[i])
            x = layer_norm(DEEPNORM_ALPHA * x + h, od_ln1_g[i], od_ln1_b[i])
            h = moe_swiglu(x, od_router_w[i], od_router_b[i], od_exp_w_gu[i], od_exp_w_down[i])
            x = layer_norm(DEEPNORM_ALPHA * x + h, od_ln2_g[i], od_ln2_b[i])
    return x
```

```python
import functools

import jax
import jax.numpy as jnp
from jax import lax
from jax.experimental import pallas as pl
from jax.experimental.pallas import tpu as pltpu

F32 = jnp.float32
BF16 = jnp.bfloat16

HEAD_DIM = 128
A_HEADS = 8
A_WIDTH = A_HEADS * HEAD_DIM
A_BRANCHES = ((128, 1), (512, 4), (2048, 16))
SSD_HEADS = 16
SSD_HEAD_DIM = 64
SSD_WIDTH = SSD_HEADS * SSD_HEAD_DIM
SSD_GROUPS = 4
SSD_STATE = 128
SSD_CONV = 4
SSD_CHUNK = 128
C_HEADS = 16
MOBA_BLOCK = 256
MOBA_TOPK = 3
N_EXPERTS = 8
LN_EPS = 1e-5
RMS_EPS = 1e-5
NEG = -1e30

LANES = 128
VMEM_LIMIT = 56 * 1024 * 1024


def _params(*sem):
    return pltpu.CompilerParams(dimension_semantics=sem, vmem_limit_bytes=VMEM_LIMIT)


def _silu(x):
    return x * (1.0 / (1.0 + jnp.exp(-x)))


def _layer_norm(z, g, b):
    mu = jnp.mean(z, axis=-1, keepdims=True)
    zc = z - mu
    var = jnp.mean(zc * zc, axis=-1, keepdims=True)
    return zc * lax.rsqrt(var + LN_EPS) * g + b


def _mm_kernel(a_ref, w_ref, o_ref):
    o_ref[...] = jnp.dot(a_ref[...], w_ref[...], preferred_element_type=F32).astype(o_ref.dtype)


def matmul(a, w, *, tm, tn, out_dtype):
    m, k = a.shape
    n = w.shape[1]
    return pl.pallas_call(
        _mm_kernel,
        out_shape=jax.ShapeDtypeStruct((m, n), out_dtype),
        grid=(n // tn, m // tm),
        in_specs=[pl.BlockSpec((tm, k), lambda j, i: (i, 0)),
                  pl.BlockSpec((k, tn), lambda j, i: (0, j))],
        out_specs=pl.BlockSpec((tm, tn), lambda j, i: (i, j)),
        compiler_params=_params("parallel", "parallel"),
        name="matmul",
    )(a, w)


def _swiglu_kernel(a_ref, wg_ref, wu_ref, o_ref):
    a = a_ref[...]
    g = jnp.dot(a, wg_ref[...], preferred_element_type=F32)
    u = jnp.dot(a, wu_ref[...], preferred_element_type=F32)
    o_ref[...] = (_silu(g) * u).astype(o_ref.dtype)


def matmul_swiglu(a, w_gu, *, tm, tn):
    m, k = a.shape
    f = w_gu.shape[1] // 2
    nf = f // tn
    return pl.pallas_call(
        _swiglu_kernel,
        out_shape=jax.ShapeDtypeStruct((m, f), BF16),
        grid=(nf, m // tm),
        in_specs=[pl.BlockSpec((tm, k), lambda j, i: (i, 0)),
                  pl.BlockSpec((k, tn), lambda j, i: (0, j)),
                  pl.BlockSpec((k, tn), lambda j, i: (0, j + nf))],
        out_specs=pl.BlockSpec((tm, tn), lambda j, i: (i, j)),
        compiler_params=_params("parallel", "parallel"),
        name="matmul_swiglu",
    )(a, w_gu, w_gu)


def _res_ln_kernel(a_ref, w_ref, x_ref, g_ref, b_ref, o_ref, ob_ref, acc_ref, *, nk, alpha):
    k = pl.program_id(1)

    @pl.when(k == 0)
    def _():
        acc_ref[...] = alpha * x_ref[...]

    acc_ref[...] += jnp.dot(a_ref[...], w_ref[...], preferred_element_type=F32)

    @pl.when(k == nk - 1)
    def _():
        y = _layer_norm(acc_ref[...], g_ref[...], b_ref[...])
        o_ref[...] = y
        ob_ref[...] = y.astype(BF16)


def matmul_res_ln(a, w, x, g, b, *, tm, tk, alpha):
    m, k = a.shape
    d = w.shape[1]
    nk = k // tk
    return pl.pallas_call(
        functools.partial(_res_ln_kernel, nk=nk, alpha=alpha),
        out_shape=(jax.ShapeDtypeStruct((m, d), F32), jax.ShapeDtypeStruct((m, d), BF16)),
        grid=(m // tm, nk),
        in_specs=[pl.BlockSpec((tm, tk), lambda i, kk: (i, kk)),
                  pl.BlockSpec((tk, d), lambda i, kk: (kk, 0)),
                  pl.BlockSpec((tm, d), lambda i, kk: (i, 0)),
                  pl.BlockSpec((1, d), lambda i, kk: (0, 0)),
                  pl.BlockSpec((1, d), lambda i, kk: (0, 0))],
        out_specs=(pl.BlockSpec((tm, d), lambda i, kk: (i, 0)),
                   pl.BlockSpec((tm, d), lambda i, kk: (i, 0))),
        scratch_shapes=[pltpu.VMEM((tm, d), F32)],
        compiler_params=_params("parallel", "arbitrary"),
        name="matmul_res_ln",
    )(a, w, x, g.reshape(1, d), b.reshape(1, d))


def _dilated_kernel(q_ref, k_ref, v_ref, o_ref, qf, kf, vf, of, lf, *, seq):
    qf[...] = q_ref[...].astype(F32)
    kf[...] = k_ref[...].astype(F32)
    vf[...] = v_ref[...].astype(F32)
    scale = HEAD_DIM ** -0.5

    for bi, (window, dil) in enumerate(A_BRANCHES):
        band = window // dil
        sub_len = seq // dil
        n_blk = sub_len // band
        has_prev = n_blk > 1
        nkeys = 2 * band if has_prev else band
        qi = lax.broadcasted_iota(jnp.int32, (band, nkeys), 0)
        kj = lax.broadcasted_iota(jnp.int32, (band, nkeys), 1)
        if has_prev:
            dist = band + qi - kj
        else:
            dist = qi - kj
        in_win = (dist >= 0) & (dist <= band)

        def block(idx, carry, *, bi=bi, dil=dil, band=band, n_blk=n_blk, has_prev=has_prev,
                  in_win=in_win, kj=kj):
            res = idx % dil
            n = idx // dil
            start = res + n * (band * dil)
            rows = pl.ds(start, band, stride=dil)
            qb = qf[rows, :].astype(BF16)
            kb = kf[rows, :]
            vb = vf[rows, :]
            mask = in_win
            if has_prev:
                prev_start = jnp.maximum(start - band * dil, res)
                prows = pl.ds(prev_start, band, stride=dil)
                kb = jnp.concatenate([kf[prows, :], kb], axis=0)
                vb = jnp.concatenate([vf[prows, :], vb], axis=0)
                mask = mask & (kj >= jnp.where(n > 0, 0, band))
            s = lax.dot_general(qb, kb.astype(BF16), (((1,), (1,)), ((), ())),
                                preferred_element_type=F32) * scale
            s = jnp.where(mask, s, NEG)
            m = jnp.max(s, axis=-1, keepdims=True)
            p = jnp.exp(s - m)
            den = jnp.sum(p, axis=-1, keepdims=True)
            o = jnp.dot(p.astype(BF16), vb.astype(BF16), preferred_element_type=F32) / den
            lse = m + jnp.log(den)
            of[bi, rows, :] = o
            lf[bi, rows, :] = jnp.broadcast_to(lse, (band, HEAD_DIM))
            return carry

        lax.fori_loop(0, dil * n_blk, block, 0)

    l0, l1, l2 = lf[0], lf[1], lf[2]
    mx = jnp.maximum(jnp.maximum(l0, l1), l2)
    w0, w1, w2 = jnp.exp(l0 - mx), jnp.exp(l1 - mx), jnp.exp(l2 - mx)
    out = (w0 * of[0] + w1 * of[1] + w2 * of[2]) / (w0 + w1 + w2)
    o_ref[...] = out.astype(o_ref.dtype)


def dilated_attention(proj, *, bsz, seq):
    kern = functools.partial(_dilated_kernel, seq=seq)
    blk = lambda off: pl.BlockSpec((None, seq, HEAD_DIM), lambda b, h, off=off: (b, 0, h + off))
    return pl.pallas_call(
        kern,
        out_shape=jax.ShapeDtypeStruct((bsz, seq, A_WIDTH), BF16),
        grid=(bsz, A_HEADS),
        in_specs=[blk(0), blk(A_HEADS), blk(2 * A_HEADS)],
        out_specs=pl.BlockSpec((None, seq, HEAD_DIM), lambda b, h: (b, 0, h)),
        scratch_shapes=[pltpu.VMEM((seq, HEAD_DIM), F32)] * 3
                       + [pltpu.VMEM((len(A_BRANCHES), seq, HEAD_DIM), F32)] * 2,
        compiler_params=_params("parallel", "parallel"),
        name="dilated_attention",
    )(proj, proj, proj)


def _ssd_kernel(z_ref, xbc_ref, dt_ref, cw_ref, cb_ref, dtb_ref, alog_ref, dskip_ref, nw_ref,
                o_ref, tail, hstate):
    q = SSD_CHUNK
    c = pl.program_id(1)

    @pl.when(c == 0)
    def _():
        tail[...] = jnp.zeros_like(tail)
        hstate[...] = jnp.zeros_like(hstate)

    x_cur = xbc_ref[...].astype(F32)
    xe = jnp.concatenate([tail[...], x_cur], axis=0)
    tail[...] = x_cur[q - 8:, :]
    conv = cb_ref[...]
    for j in range(SSD_CONV):
        off = 8 - (SSD_CONV - 1) + j
        conv = conv + cw_ref[j:j + 1, :] * xe[off:off + q, :]
    xbc = _silu(conv)
    gn = SSD_GROUPS * SSD_STATE
    xs = xbc[:, :SSD_WIDTH]
    bm = xbc[:, SSD_WIDTH:SSD_WIDTH + gn]
    cm = xbc[:, SSD_WIDTH + gn:]

    dt_in = dt_ref[...] + dtb_ref[...]
    dt = jnp.maximum(dt_in, 0.0) + jnp.log1p(jnp.exp(-jnp.abs(dt_in)))
    a_head = -jnp.exp(alog_ref[...])
    dta = dt * a_head
    li = lax.broadcasted_iota(jnp.int32, (q, q), 0)
    si = lax.broadcasted_iota(jnp.int32, (q, q), 1)
    causal = li >= si
    a_cs = jnp.dot(causal.astype(F32), dta, preferred_element_type=F32,
                   precision=lax.Precision.HIGHEST)
    a_cs_t = a_cs.T

    bm_b = bm.astype(BF16)
    cm_b = cm.astype(BF16)
    rep = SSD_HEADS // SSD_GROUPS
    ys = []
    for g in range(SSD_GROUPS):
        bg = bm_b[:, g * SSD_STATE:(g + 1) * SSD_STATE]
        cg = cm_b[:, g * SSD_STATE:(g + 1) * SSD_STATE]
        cbg = lax.dot_general(cg, bg, (((1,), (1,)), ((), ())), preferred_element_type=F32)
        bg_t = bm[:, g * SSD_STATE:(g + 1) * SSD_STATE].T
        for r in range(rep):
            h = g * rep + r
            col = a_cs[:, h:h + 1]
            row = a_cs_t[h:h + 1, :]
            decay_in = jnp.exp(jnp.where(causal, col - row, NEG))
            xh = xs[:, h * SSD_HEAD_DIM:(h + 1) * SSD_HEAD_DIM]
            xdt = (xh * dt[:, h:h + 1]).astype(BF16)
            y = jnp.dot((cbg * decay_in).astype(BF16), xdt, preferred_element_type=F32)
            h_prev = hstate[h]
            y_off = jnp.dot(cg, h_prev.astype(BF16), preferred_element_type=F32) * jnp.exp(col)
            last = a_cs_t[h:h + 1, q - 1:q]
            decay_st = jnp.exp(last - row)
            st = jnp.dot((bg_t * decay_st).astype(BF16), xdt, preferred_element_type=F32)
            hstate[h] = jnp.exp(last) * h_prev + st
            ys.append(y + y_off)
    y = jnp.concatenate(ys, axis=-1) + dskip_ref[...] * xs
    y = y * _silu(z_ref[...].astype(F32))
    gw = SSD_WIDTH // SSD_GROUPS
    outs = []
    for g in range(SSD_GROUPS):
        yg = y[:, g * gw:(g + 1) * gw]
        outs.append(yg * lax.rsqrt(jnp.mean(yg * yg, axis=-1, keepdims=True) + RMS_EPS))
    o_ref[...] = (jnp.concatenate(outs, axis=-1) * nw_ref[...]).astype(o_ref.dtype)


def ssd_mixer(proj, dt_raw, conv_w, conv_b, dt_bias, a_log, d_skip, norm_w, *, bsz, seq):
    cw = SSD_WIDTH + 2 * SSD_GROUPS * SSD_STATE
    pad = lambda v: jnp.pad(v.astype(F32), (0, LANES - SSD_HEADS)).reshape(1, LANES)
    dskip = jnp.repeat(d_skip.astype(F32), SSD_HEAD_DIM).reshape(1, SSD_WIDTH)
    const = lambda shape: pl.BlockSpec(shape, lambda b, c: (0, 0))
    z_blk = (3 * A_WIDTH) // SSD_WIDTH
    xbc_blk = (3 * A_WIDTH + SSD_WIDTH) // cw
    return pl.pallas_call(
        _ssd_kernel,
        out_shape=jax.ShapeDtypeStruct((bsz, seq, SSD_WIDTH), BF16),
        grid=(bsz, seq // SSD_CHUNK),
        in_specs=[pl.BlockSpec((None, SSD_CHUNK, SSD_WIDTH), lambda b, c: (b, c, z_blk)),
                  pl.BlockSpec((None, SSD_CHUNK, cw), lambda b, c: (b, c, xbc_blk)),
                  pl.BlockSpec((None, SSD_CHUNK, LANES), lambda b, c: (b, c, 0)),
                  const((SSD_CONV, cw)), const((1, cw)), const((1, LANES)), const((1, LANES)),
                  const((1, SSD_WIDTH)), const((1, SSD_WIDTH))],
        out_specs=pl.BlockSpec((None, SSD_CHUNK, SSD_WIDTH), lambda b, c: (b, c, 0)),
        scratch_shapes=[pltpu.VMEM((8, cw), F32),
                        pltpu.VMEM((SSD_HEADS, SSD_STATE, SSD_HEAD_DIM), F32)],
        compiler_params=_params("parallel", "arbitrary"),
        name="ssd_mixer",
    )(proj, proj, dt_raw, conv_w.astype(F32), conv_b.astype(F32).reshape(1, cw), pad(dt_bias),
      pad(a_log), dskip, norm_w.astype(F32).reshape(1, SSD_WIDTH))


def _moba_kernel(q_ref, k_ref, v_ref, o_ref, *, seq):
    nb = seq // MOBA_BLOCK
    blk = MOBA_BLOCK
    scale = HEAD_DIM ** -0.5
    n_sel = min(MOBA_TOPK, nb)

    kmeans = [jnp.mean(k_ref[n * blk:(n + 1) * blk, :].astype(F32), axis=0, keepdims=True)
              for n in range(nb)]
    kmean = jnp.concatenate(kmeans + [jnp.zeros((LANES - nb, HEAD_DIM), F32)], axis=0)
    lane = lax.broadcasted_iota(jnp.int32, (blk, LANES), 1)
    qi = lax.broadcasted_iota(jnp.int32, (blk, blk), 0)
    kj = lax.broadcasted_iota(jnp.int32, (blk, blk), 1)
    causal = kj <= qi

    for j in range(nb):
        qb = q_ref[j * blk:(j + 1) * blk, :]
        s_self = lax.dot_general(qb, k_ref[j * blk:(j + 1) * blk, :], (((1,), (1,)), ((), ())),
                                 preferred_element_type=F32) * scale
        scores = [jnp.where(causal, s_self, NEG)]
        if j > 0:
            gate = lax.dot_general(qb.astype(F32), kmean, (((1,), (1,)), ((), ())),
                                   preferred_element_type=F32, precision=lax.Precision.HIGHEST)
            gate = jnp.where(lane < j, gate, NEG)
            for n in range(j):
                gn = gate[:, n:n + 1]
                beats = (gate > gn) | ((gate == gn) & (lane < n))
                rank = jnp.sum(beats.astype(F32), axis=-1, keepdims=True)
                chosen = rank < n_sel
                s = lax.dot_general(qb, k_ref[n * blk:(n + 1) * blk, :], (((1,), (1,)), ((), ())),
                                    preferred_element_type=F32) * scale
                scores.append(jnp.where(chosen, s, NEG))
        m = scores[0].max(axis=-1, keepdims=True)
        for s in scores[1:]:
            m = jnp.maximum(m, s.max(axis=-1, keepdims=True))
        den = jnp.zeros((blk, 1), F32)
        acc = jnp.zeros((blk, HEAD_DIM), F32)
        for idx, s in enumerate(scores):
            n = j if idx == 0 else idx - 1
            p = jnp.exp(s - m)
            den = den + jnp.sum(p, axis=-1, keepdims=True)
            acc = acc + jnp.dot(p.astype(BF16), v_ref[n * blk:(n + 1) * blk, :],
                                preferred_element_type=F32)
        o_ref[j * blk:(j + 1) * blk, :] = (acc / den).astype(o_ref.dtype)


def moba_attention(qkv, *, bsz, seq):
    blk = lambda off: pl.BlockSpec((None, seq, HEAD_DIM), lambda b, h, off=off: (b, 0, h + off))
    return pl.pallas_call(
        functools.partial(_moba_kernel, seq=seq),
        out_shape=jax.ShapeDtypeStruct((bsz, seq, C_HEADS * HEAD_DIM), BF16),
        grid=(bsz, C_HEADS),
        in_specs=[blk(0), blk(C_HEADS), blk(2 * C_HEADS)],
        out_specs=pl.BlockSpec((None, seq, HEAD_DIM), lambda b, h: (b, 0, h)),
        compiler_params=_params("parallel", "parallel"),
        name="moba_attention",
    )(qkv, qkv, qkv)


def _router_kernel(x_ref, w_ref, b_ref, idx_ref, gate_ref):
    logits = jnp.dot(x_ref[...], w_ref[...], preferred_element_type=F32,
                     precision=lax.Precision.HIGHEST) + b_ref[...]
    tm = logits.shape[0]
    lane = lax.broadcasted_iota(jnp.int32, (tm, LANES), 1)
    logits = jnp.where(lane < N_EXPERTS, logits, NEG)
    v1 = jnp.max(logits, axis=-1, keepdims=True)
    i1 = jnp.min(jnp.where(logits == v1, lane, LANES), axis=-1, keepdims=True)
    rest = jnp.where(lane == i1, NEG, logits)
    v2 = jnp.max(rest, axis=-1, keepdims=True)
    i2 = jnp.min(jnp.where(rest == v2, lane, LANES), axis=-1, keepdims=True)
    e2 = jnp.exp(v2 - v1)
    g1 = 1.0 / (1.0 + e2)
    g2 = e2 / (1.0 + e2)
    idx_ref[...] = jnp.where(lane == 0, i1, jnp.where(lane == 1, i2, 0))
    gate_ref[...] = jnp.where(lane == 0, g1, jnp.where(lane == 1, g2, 0.0))


def moe_router(x, w_router, b_router, *, tm):
    t, d = x.shape
    w = jnp.pad(w_router.astype(F32), ((0, 0), (0, LANES - N_EXPERTS)))
    b = jnp.pad(b_router.astype(F32), (0, LANES - N_EXPERTS)).reshape(1, LANES)
    return pl.pallas_call(
        _router_kernel,
        out_shape=(jax.ShapeDtypeStruct((t, LANES), jnp.int32), jax.ShapeDtypeStruct((t, LANES), F32)),
        grid=(t // tm,),
        in_specs=[pl.BlockSpec((tm, d), lambda i: (i, 0)),
                  pl.BlockSpec((d, LANES), lambda i: (0, 0)),
                  pl.BlockSpec((1, LANES), lambda i: (0, 0))],
        out_specs=(pl.BlockSpec((tm, LANES), lambda i: (i, 0)),
                   pl.BlockSpec((tm, LANES), lambda i: (i, 0))),
        compiler_params=_params("parallel"),
        name="moe_router",
    )(x, w, b)


def _gather_kernel(src_ref, x_hbm, o_ref, buf, sem, *, rows):
    def row_copy(i):
        return pltpu.make_async_copy(x_hbm.at[pl.ds(src_ref[0, 0, i], 1)], buf.at[pl.ds(i, 1)], sem)

    def start(i, c):
        row_copy(i).start()
        return c

    def wait(i, c):
        row_copy(i).wait()
        return c

    lax.fori_loop(0, rows, start, 0)
    lax.fori_loop(0, rows, wait, 0)
    o_ref[...] = buf[...].astype(o_ref.dtype)


def gather_rows(x, src, *, rows):
    n = src.shape[0]
    d = x.shape[1]
    return pl.pallas_call(
        functools.partial(_gather_kernel, rows=rows),
        out_shape=jax.ShapeDtypeStruct((n, d), BF16),
        grid=(n // rows,),
        in_specs=[pl.BlockSpec((1, 1, rows), lambda i: (i, 0, 0), memory_space=pltpu.SMEM),
                  pl.BlockSpec(memory_space=pl.ANY)],
        out_specs=pl.BlockSpec((rows, d), lambda i: (i, 0)),
        scratch_shapes=[pltpu.VMEM((rows, d), F32), pltpu.SemaphoreType.DMA],
        compiler_params=_params("arbitrary"),
        name="moe_gather",
    )(src.reshape(n // rows, 1, rows), x)


def _grouped_swiglu_kernel(be_ref, nu_ref, a_ref, wg_ref, wu_ref, o_ref):
    @pl.when(pl.program_id(1) < nu_ref[0])
    def _():
        a = a_ref[...]
        g = jnp.dot(a, wg_ref[...], preferred_element_type=F32)
        u = jnp.dot(a, wu_ref[...], preferred_element_type=F32)
        o_ref[...] = (_silu(g) * u).astype(o_ref.dtype)

    @pl.when(pl.program_id(1) >= nu_ref[0])
    def _():
        o_ref[...] = jnp.zeros_like(o_ref)


def grouped_swiglu(a, w_gu, block_expert, n_used, *, tm, tn):
    n, k = a.shape
    f = w_gu.shape[2] // 2
    nf = f // tn
    grid_spec = pltpu.PrefetchScalarGridSpec(
        num_scalar_prefetch=2,
        grid=(nf, n // tm),
        in_specs=[pl.BlockSpec((tm, k), lambda j, i, be, nu: (i, 0)),
                  pl.BlockSpec((None, k, tn), lambda j, i, be, nu: (be[i], 0, j)),
                  pl.BlockSpec((None, k, tn), lambda j, i, be, nu: (be[i], 0, j + nf))],
        out_specs=pl.BlockSpec((tm, tn), lambda j, i, be, nu: (i, j)),
    )
    return pl.pallas_call(
        _grouped_swiglu_kernel,
        out_shape=jax.ShapeDtypeStruct((n, f), BF16),
        grid_spec=grid_spec,
        compiler_params=_params("parallel", "arbitrary"),
        name="moe_grouped_swiglu",
    )(block_expert, n_used, a, w_gu, w_gu)


def _grouped_down_kernel(be_ref, nu_ref, a_ref, w_ref, o_ref):
    @pl.when(pl.program_id(1) < nu_ref[0])
    def _():
        o_ref[...] = jnp.dot(a_ref[...], w_ref[...], preferred_element_type=F32)

    @pl.when(pl.program_id(1) >= nu_ref[0])
    def _():
        o_ref[...] = jnp.zeros_like(o_ref)


def grouped_down(a, w_down, block_expert, n_used, *, tm, tn):
    n, f = a.shape
    d = w_down.shape[2]
    grid_spec = pltpu.PrefetchScalarGridSpec(
        num_scalar_prefetch=2,
        grid=(d // tn, n // tm),
        in_specs=[pl.BlockSpec((tm, f), lambda j, i, be, nu: (i, 0)),
                  pl.BlockSpec((None, f, tn), lambda j, i, be, nu: (be[i], 0, j))],
        out_specs=pl.BlockSpec((tm, tn), lambda j, i, be, nu: (i, j)),
    )
    return pl.pallas_call(
        _grouped_down_kernel,
        out_shape=jax.ShapeDtypeStruct((n, d), F32),
        grid_spec=grid_spec,
        compiler_params=_params("parallel", "arbitrary"),
        name="moe_grouped_down",
    )(block_expert, n_used, a, w_down)


def _combine_kernel(slot_ref, y_hbm, gate_ref, x_ref, g_ref, b_ref, o_ref, ob_ref, buf, sem,
                    *, rows, alpha):
    def row_copy(i, k):
        return pltpu.make_async_copy(y_hbm.at[pl.ds(slot_ref[0, k, i], 1)],
                                     buf.at[k, pl.ds(i, 1)], sem)

    def start(i, c):
        row_copy(i, 0).start()
        row_copy(i, 1).start()
        return c

    def wait(i, c):
        row_copy(i, 0).wait()
        row_copy(i, 1).wait()
        return c

    lax.fori_loop(0, rows, start, 0)
    lax.fori_loop(0, rows, wait, 0)
    gates = gate_ref[...]
    h = gates[:, 0:1] * buf[0] + gates[:, 1:2] * buf[1]
    y = _layer_norm(alpha * x_ref[...] + h, g_ref[...], b_ref[...])
    o_ref[...] = y
    ob_ref[...] = y.astype(BF16)


def moe_combine_ln(y, slots, gates, x, g, b, *, rows, alpha):
    t, d = x.shape
    return pl.pallas_call(
        functools.partial(_combine_kernel, rows=rows, alpha=alpha),
        out_shape=(jax.ShapeDtypeStruct((t, d), F32), jax.ShapeDtypeStruct((t, d), BF16)),
        grid=(t // rows,),
        in_specs=[pl.BlockSpec((1, 2, rows), lambda i: (i, 0, 0), memory_space=pltpu.SMEM),
                  pl.BlockSpec(memory_space=pl.ANY),
                  pl.BlockSpec((rows, LANES), lambda i: (i, 0)),
                  pl.BlockSpec((rows, d), lambda i: (i, 0)),
                  pl.BlockSpec((1, d), lambda i: (0, 0)),
                  pl.BlockSpec((1, d), lambda i: (0, 0))],
        out_specs=(pl.BlockSpec((rows, d), lambda i: (i, 0)),
                   pl.BlockSpec((rows, d), lambda i: (i, 0))),
        scratch_shapes=[pltpu.VMEM((2, rows, d), F32), pltpu.SemaphoreType.DMA],
        compiler_params=_params("arbitrary"),
        name="moe_combine_ln",
    )(slots.reshape(2, t // rows, rows).transpose(1, 0, 2), y, gates, x, g.reshape(1, d), b.reshape(1, d))


def moe_dispatch_tables(top_idx, *, tm):
    t = top_idx.shape[0]
    n_rows = _padded_rows(t, tm)
    flat = top_idx.reshape(-1)
    onehot = (flat[:, None] == jnp.arange(N_EXPERTS)[None, :]).astype(jnp.int32)
    csum = jnp.cumsum(onehot, axis=0)
    rank = jnp.sum(csum * onehot, axis=1) - 1
    counts = csum[-1]
    padded = ((counts + tm - 1) // tm) * tm
    ends = jnp.cumsum(padded)
    starts = ends - padded
    dest = starts[flat] + rank
    src = jnp.zeros((n_rows,), jnp.int32).at[dest].set(jnp.arange(2 * t, dtype=jnp.int32) // 2)
    blk_start = jnp.arange(n_rows // tm, dtype=jnp.int32) * tm
    block_expert = jnp.minimum(jnp.sum(blk_start[:, None] >= ends[None, :], axis=1),
                               N_EXPERTS - 1).astype(jnp.int32)
    n_used = (ends[-1] // tm).astype(jnp.int32).reshape(1)
    slots = dest.reshape(t, 2).T.astype(jnp.int32)
    return src, slots, block_expert, n_used


def _padded_rows(t, tm):
    return 2 * t + N_EXPERTS * tm


def moe_layer(x, w_router, b_router, w_gu, w_down, g, b, *, alpha):
    tm = 512
    idx, gates = moe_router(x, w_router, b_router, tm=512)
    src, slots, block_expert, n_used = moe_dispatch_tables(idx[:, :2], tm=tm)
    xs = gather_rows(x, src, rows=256)
    h = grouped_swiglu(xs, w_gu, block_expert, n_used, tm=tm, tn=512)
    y = grouped_down(h, w_down, block_expert, n_used, tm=tm, tn=1024)
    return moe_combine_ln(y, slots, gates, x, g, b, rows=256, alpha=alpha)


def kernel(x, ev_w_in, ev_conv_w, ev_conv_b, ev_dt_bias, ev_a_log, ev_d_skip, ev_ssd_norm, ev_w_out, ev_ln1_g, ev_ln1_b, ev_ffn_w_gu, ev_ffn_w_down, ev_ln2_g, ev_ln2_b, od_w_qkv, od_w_out, od_ln1_g, od_ln1_b, od_router_w, od_router_b, od_exp_w_gu, od_exp_w_down, od_ln2_g, od_ln2_b):
    bsz, seq, d = x.shape
    t = bsz * seq
    depth = ev_w_in.shape[0] + od_w_qkv.shape[0]
    alpha = (2 * depth) ** 0.25
    qkvz = 3 * A_WIDTH + SSD_WIDTH + SSD_WIDTH + 2 * SSD_GROUPS * SSD_STATE

    xf = x.reshape(t, d).astype(F32)
    xb = xf.astype(BF16)
    for layer in range(depth):
        i = layer // 2
        if layer % 2 == 0:
            w_in = ev_w_in[i]
            proj = matmul(xb, w_in[:, :qkvz].astype(BF16), tm=1024, tn=1024, out_dtype=BF16)
            w_dt = jnp.pad(w_in[:, qkvz:], ((0, 0), (0, LANES - SSD_HEADS))).astype(BF16)
            dt_raw = matmul(xb, w_dt, tm=2048, tn=LANES, out_dtype=F32)
            proj = proj.reshape(bsz, seq, qkvz)
            y_a = dilated_attention(proj, bsz=bsz, seq=seq)
            y_b = ssd_mixer(proj, dt_raw.reshape(bsz, seq, LANES), ev_conv_w[i], ev_conv_b[i],
                            ev_dt_bias[i], ev_a_log[i], ev_d_skip[i], ev_ssd_norm[i],
                            bsz=bsz, seq=seq)
            mix = jnp.concatenate([y_a, y_b], axis=-1).reshape(t, A_WIDTH + SSD_WIDTH)
            xf, xb = matmul_res_ln(mix, ev_w_out[i].astype(BF16), xf, ev_ln1_g[i], ev_ln1_b[i],
                                   tm=512, tk=mix.shape[1], alpha=alpha)
            hid = matmul_swiglu(xb, ev_ffn_w_gu[i].astype(BF16), tm=1024, tn=512)
            xf, xb = matmul_res_ln(hid, ev_ffn_w_down[i].astype(BF16), xf, ev_ln2_g[i], ev_ln2_b[i],
                                   tm=512, tk=1408, alpha=alpha)
        else:
            qkv = matmul(xb, od_w_qkv[i].astype(BF16), tm=1024, tn=1024, out_dtype=BF16)
            att = moba_attention(qkv.reshape(bsz, seq, 3 * C_HEADS * HEAD_DIM), bsz=bsz, seq=seq)
            xf, xb = matmul_res_ln(att.reshape(t, C_HEADS * HEAD_DIM), od_w_out[i].astype(BF16), xf,
                                   od_ln1_g[i], od_ln1_b[i], tm=512, tk=C_HEADS * HEAD_DIM, alpha=alpha)
            xf, xb = moe_layer(xf, od_router_w[i], od_router_b[i], od_exp_w_gu[i].astype(BF16),
                               od_exp_w_down[i].astype(BF16), od_ln2_g[i], od_ln2_b[i], alpha=alpha)
    return xf.reshape(bsz, seq, d).astype(x.dtype)
```

```python
import functools

import jax
import jax.numpy as jnp
from jax import lax
from jax.experimental import pallas as pl
from jax.experimental.pallas import tpu as pltpu

F32 = jnp.float32
BF16 = jnp.bfloat16

HEAD_DIM = 128
A_HEADS = 8
A_WIDTH = A_HEADS * HEAD_DIM
A_BRANCHES = ((128, 1), (512, 4), (2048, 16))
SSD_HEADS = 16
SSD_HEAD_DIM = 64
SSD_WIDTH = SSD_HEADS * SSD_HEAD_DIM
SSD_GROUPS = 4
SSD_STATE = 128
SSD_CONV = 4
SSD_CHUNK = 128
C_HEADS = 16
MOBA_BLOCK = 256
MOBA_TOPK = 3
N_EXPERTS = 8
LN_EPS = 1e-5
RMS_EPS = 1e-5
NEG = -1e30
LOG2E = 1.4426950408889634

LANES = 128
VMEM_LIMIT = 56 * 1024 * 1024


def _params(*sem):
    return pltpu.CompilerParams(dimension_semantics=sem, vmem_limit_bytes=VMEM_LIMIT)


def _silu(x):
    return x * (1.0 / (1.0 + jnp.exp(-x)))


def _layer_norm(z, g, b):
    mu = jnp.mean(z, axis=-1, keepdims=True)
    zc = z - mu
    var = jnp.mean(zc * zc, axis=-1, keepdims=True)
    return zc * lax.rsqrt(var + LN_EPS) * g + b


def _mm_kernel(a_ref, w_ref, o_ref, wb_ref):
    @pl.when(pl.program_id(1) == 0)
    def _():
        wb_ref[...] = w_ref[...].astype(BF16)

    o_ref[...] = jnp.dot(a_ref[...], wb_ref[...], preferred_element_type=F32).astype(o_ref.dtype)


def matmul(a, w, layer, *, n_out, tm, tn, out_dtype):
    m, k = a.shape
    return pl.pallas_call(
        _mm_kernel,
        out_shape=jax.ShapeDtypeStruct((m, n_out), out_dtype),
        grid=(n_out // tn, m // tm),
        in_specs=[pl.BlockSpec((tm, k), lambda j, i: (i, 0)),
                  pl.BlockSpec((None, k, tn), lambda j, i: (layer, 0, j))],
        out_specs=pl.BlockSpec((tm, tn), lambda j, i: (i, j)),
        scratch_shapes=[pltpu.VMEM((k, tn), BF16)],
        compiler_params=_params("parallel", "arbitrary"),
        name="matmul",
    )(a, w)


def _swiglu_kernel(a_ref, wg_ref, wu_ref, o_ref, wgb_ref, wub_ref):
    @pl.when(pl.program_id(1) == 0)
    def _():
        wgb_ref[...] = wg_ref[...].astype(BF16)
        wub_ref[...] = wu_ref[...].astype(BF16)

    a = a_ref[...]
    g = jnp.dot(a, wgb_ref[...], preferred_element_type=F32)
    u = jnp.dot(a, wub_ref[...], preferred_element_type=F32)
    o_ref[...] = (_silu(g) * u).astype(o_ref.dtype)


def matmul_swiglu(a, w_gu, layer, *, tm, tn):
    m, k = a.shape
    f = w_gu.shape[2] // 2
    nf = f // tn
    return pl.pallas_call(
        _swiglu_kernel,
        out_shape=jax.ShapeDtypeStruct((m, f), BF16),
        grid=(nf, m // tm),
        in_specs=[pl.BlockSpec((tm, k), lambda j, i: (i, 0)),
                  pl.BlockSpec((None, k, tn), lambda j, i: (layer, 0, j)),
                  pl.BlockSpec((None, k, tn), lambda j, i: (layer, 0, j + nf))],
        out_specs=pl.BlockSpec((tm, tn), lambda j, i: (i, j)),
        scratch_shapes=[pltpu.VMEM((k, tn), BF16)] * 2,
        compiler_params=_params("parallel", "arbitrary"),
        name="matmul_swiglu",
    )(a, w_gu, w_gu)


def _res_ln_kernel(a_ref, w_ref, x_ref, g_ref, b_ref, o_ref, ob_ref, acc_ref, *, nk, alpha):
    k = pl.program_id(1)

    @pl.when(k == 0)
    def _():
        acc_ref[...] = alpha * x_ref[...]

    acc_ref[...] += jnp.dot(a_ref[...], w_ref[...], preferred_element_type=F32)

    @pl.when(k == nk - 1)
    def _():
        y = _layer_norm(acc_ref[...], g_ref[...], b_ref[...])
        o_ref[...] = y
        ob_ref[...] = y.astype(BF16)


def matmul_res_ln(a, w, x, g, b, *, tm, tk, alpha):
    m, k = a.shape
    d = w.shape[1]
    nk = k // tk
    return pl.pallas_call(
        functools.partial(_res_ln_kernel, nk=nk, alpha=alpha),
        out_shape=(jax.ShapeDtypeStruct((m, d), F32), jax.ShapeDtypeStruct((m, d), BF16)),
        grid=(m // tm, nk),
        in_specs=[pl.BlockSpec((tm, tk), lambda i, kk: (i, kk)),
                  pl.BlockSpec((tk, d), lambda i, kk: (kk, 0)),
                  pl.BlockSpec((tm, d), lambda i, kk: (i, 0)),
                  pl.BlockSpec((1, d), lambda i, kk: (0, 0)),
                  pl.BlockSpec((1, d), lambda i, kk: (0, 0))],
        out_specs=(pl.BlockSpec((tm, d), lambda i, kk: (i, 0)),
                   pl.BlockSpec((tm, d), lambda i, kk: (i, 0))),
        scratch_shapes=[pltpu.VMEM((tm, d), F32)],
        compiler_params=_params("parallel", "arbitrary"),
        name="matmul_res_ln",
    )(a, w, x, g.reshape(1, d), b.reshape(1, d))


def _dilated_kernel(q_ref, k_ref, v_ref, o_ref, qf, kf, vf, qg, kg, vg, of, lf, mf, *, seq):
    (w0, d0), (w1, g), (w2, d2) = A_BRANCHES
    assert d0 == 1 and d2 == g * g and w0 // d0 == w1 // g == w2 // d2
    band = w0
    sub = seq // g
    assert seq // d2 == band
    scale = HEAD_DIM ** -0.5
    nt = (((1,), (1,)), ((), ()))

    qf[...] = q_ref[...].astype(F32)
    kf[...] = k_ref[...].astype(F32)
    vf[...] = v_ref[...].astype(F32)
    for r in range(g):
        nat = pl.ds(r, sub, stride=g)
        grp = pl.ds(r * sub, sub)
        qg[grp, :] = qf[nat, :]
        kg[grp, :] = kf[nat, :]
        vg[grp, :] = vf[nat, :]

    qi = lax.broadcasted_iota(jnp.int32, (band, 2 * band), 0)
    kj = lax.broadcasted_iota(jnp.int32, (band, 2 * band), 1)
    dist = band + qi - kj
    in_win2 = (dist >= 0) & (dist <= band)
    causal = (lax.broadcasted_iota(jnp.int32, (band, band), 1)
              <= lax.broadcasted_iota(jnp.int32, (band, band), 0))

    def attend(qb, kb, vb, mask):
        s = lax.dot_general(qb.astype(BF16), kb.astype(BF16), nt, preferred_element_type=F32) * scale
        s = jnp.where(mask, s, NEG)
        m = jnp.max(s, axis=-1, keepdims=True)
        p = jnp.exp(s - m)
        den = jnp.sum(p, axis=-1, keepdims=True)
        o = jnp.dot(p.astype(BF16), vb.astype(BF16), preferred_element_type=F32) / den
        return o, jnp.broadcast_to(m + jnp.log(den), (band, HEAD_DIM))

    def banded(bi, qs, ks, vs, n_blk):
        for idx in range(seq // band):
            rows = pl.ds(idx * band, band)
            if idx % n_blk == 0:
                o, lse = attend(qs[rows, :], ks[rows, :], vs[rows, :], causal)
            else:
                both = pl.ds((idx - 1) * band, 2 * band)
                o, lse = attend(qs[rows, :], ks[both, :], vs[both, :], in_win2)
            of[bi, rows, :] = o
            lf[bi, rows, :] = lse

    banded(0, qf, kf, vf, seq // band)
    banded(1, qg, kg, vg, sub // band)
    for r in range(g):
        for m in range(g):
            rows = pl.ds(r * sub + m, band, stride=g)
            o, lse = attend(qg[rows, :], kg[rows, :], vg[rows, :], causal)
            of[2, rows, :] = o
            lf[2, rows, :] = lse

    for r in range(g):
        nat = pl.ds(r, sub, stride=g)
        grp = pl.ds(r * sub, sub)
        l0, l1, l2 = lf[0, nat, :], lf[1, grp, :], lf[2, grp, :]
        mx = jnp.maximum(jnp.maximum(l0, l1), l2)
        e0, e1, e2 = jnp.exp(l0 - mx), jnp.exp(l1 - mx), jnp.exp(l2 - mx)
        mf[nat, :] = (e0 * of[0, nat, :] + e1 * of[1, grp, :] + e2 * of[2, grp, :]) / (e0 + e1 + e2)
    o_ref[...] = mf[...].astype(o_ref.dtype)


def dilated_attention(proj, *, bsz, seq):
    kern = functools.partial(_dilated_kernel, seq=seq)
    blk = lambda off: pl.BlockSpec((None, seq, HEAD_DIM), lambda b, h, off=off: (b, 0, h + off))
    return pl.pallas_call(
        kern,
        out_shape=jax.ShapeDtypeStruct((bsz, seq, A_WIDTH), BF16),
        grid=(bsz, A_HEADS),
        in_specs=[blk(0), blk(A_HEADS), blk(2 * A_HEADS)],
        out_specs=pl.BlockSpec((None, seq, HEAD_DIM), lambda b, h: (b, 0, h)),
        scratch_shapes=[pltpu.VMEM((seq, HEAD_DIM), F32)] * 6
                       + [pltpu.VMEM((len(A_BRANCHES), seq, HEAD_DIM), F32)] * 2
                       + [pltpu.VMEM((seq, HEAD_DIM), F32)],
        compiler_params=_params("parallel", "parallel"),
        name="dilated_attention",
    )(proj, proj, proj)


def _ssd_kernel(z_ref, xbc_ref, dt_ref, cw_ref, cb_ref, dtb_ref, alog_ref, dskip_ref, nw_ref,
                o_ref, tail, hstate):
    q = SSD_CHUNK
    c = pl.program_id(1)

    @pl.when(c == 0)
    def _():
        tail[...] = jnp.zeros_like(tail)
        hstate[...] = jnp.zeros_like(hstate)

    x_cur = xbc_ref[...].astype(F32)
    xe = jnp.concatenate([tail[...], x_cur], axis=0)
    tail[...] = x_cur[q - 8:, :]
    conv = cb_ref[...]
    for j in range(SSD_CONV):
        off = 8 - (SSD_CONV - 1) + j
        conv = conv + cw_ref[j:j + 1, :] * xe[off:off + q, :]
    xbc = _silu(conv)
    gn = SSD_GROUPS * SSD_STATE
    xs = xbc[:, :SSD_WIDTH]
    bm = xbc[:, SSD_WIDTH:SSD_WIDTH + gn]
    cm = xbc[:, SSD_WIDTH + gn:]

    dt_in = dt_ref[...] + dtb_ref[...]
    dt = jnp.maximum(dt_in, 0.0) + jnp.log1p(jnp.exp(-jnp.abs(dt_in)))
    a_head = -jnp.exp(alog_ref[...])
    dta = dt * a_head
    li = lax.broadcasted_iota(jnp.int32, (q, q), 0)
    si = lax.broadcasted_iota(jnp.int32, (q, q), 1)
    causal = li >= si
    a_cs = jnp.dot(causal.astype(F32), dta, preferred_element_type=F32,
                   precision=lax.Precision.HIGHEST)
    a_cs_t = a_cs.T

    bm_b = bm.astype(BF16)
    cm_b = cm.astype(BF16)
    rep = SSD_HEADS // SSD_GROUPS
    ys = []
    for g in range(SSD_GROUPS):
        bg = bm_b[:, g * SSD_STATE:(g + 1) * SSD_STATE]
        cg = cm_b[:, g * SSD_STATE:(g + 1) * SSD_STATE]
        cbg = lax.dot_general(cg, bg, (((1,), (1,)), ((), ())), preferred_element_type=F32)
        bg_t = bm[:, g * SSD_STATE:(g + 1) * SSD_STATE].T
        for r in range(rep):
            h = g * rep + r
            col = a_cs[:, h:h + 1]
            row = a_cs_t[h:h + 1, :]
            decay_in = jnp.exp(jnp.where(causal, col - row, NEG))
            xh = xs[:, h * SSD_HEAD_DIM:(h + 1) * SSD_HEAD_DIM]
            xdt = (xh * dt[:, h:h + 1]).astype(BF16)
            y = jnp.dot((cbg * decay_in).astype(BF16), xdt, preferred_element_type=F32)
            h_prev = hstate[h]
            y_off = jnp.dot(cg, h_prev.astype(BF16), preferred_element_type=F32) * jnp.exp(col)
            last = a_cs_t[h:h + 1, q - 1:q]
            decay_st = jnp.exp(last - row)
            st = jnp.dot((bg_t * decay_st).astype(BF16), xdt, preferred_element_type=F32)
            hstate[h] = jnp.exp(last) * h_prev + st
            ys.append(y + y_off)
    y = jnp.concatenate(ys, axis=-1) + dskip_ref[...] * xs
    y = y * _silu(z_ref[...].astype(F32))
    gw = SSD_WIDTH // SSD_GROUPS
    outs = []
    for g in range(SSD_GROUPS):
        yg = y[:, g * gw:(g + 1) * gw]
        outs.append(yg * lax.rsqrt(jnp.mean(yg * yg, axis=-1, keepdims=True) + RMS_EPS))
    o_ref[...] = (jnp.concatenate(outs, axis=-1) * nw_ref[...]).astype(o_ref.dtype)


def ssd_mixer(proj, dt_raw, conv_w, conv_b, dt_bias, a_log, d_skip, norm_w, *, bsz, seq):
    cw = SSD_WIDTH + 2 * SSD_GROUPS * SSD_STATE
    pad = lambda v: jnp.pad(v.astype(F32), (0, LANES - SSD_HEADS)).reshape(1, LANES)
    dskip = jnp.repeat(d_skip.astype(F32), SSD_HEAD_DIM).reshape(1, SSD_WIDTH)
    const = lambda shape: pl.BlockSpec(shape, lambda b, c: (0, 0))
    z_blk = (3 * A_WIDTH) // SSD_WIDTH
    xbc_blk = (3 * A_WIDTH + SSD_WIDTH) // cw
    return pl.pallas_call(
        _ssd_kernel,
        out_shape=jax.ShapeDtypeStruct((bsz, seq, SSD_WIDTH), BF16),
        grid=(bsz, seq // SSD_CHUNK),
        in_specs=[pl.BlockSpec((None, SSD_CHUNK, SSD_WIDTH), lambda b, c: (b, c, z_blk)),
                  pl.BlockSpec((None, SSD_CHUNK, cw), lambda b, c: (b, c, xbc_blk)),
                  pl.BlockSpec((None, SSD_CHUNK, LANES), lambda b, c: (b, c, 0)),
                  const((SSD_CONV, cw)), const((1, cw)), const((1, LANES)), const((1, LANES)),
                  const((1, SSD_WIDTH)), const((1, SSD_WIDTH))],
        out_specs=pl.BlockSpec((None, SSD_CHUNK, SSD_WIDTH), lambda b, c: (b, c, 0)),
        scratch_shapes=[pltpu.VMEM((8, cw), F32),
                        pltpu.VMEM((SSD_HEADS, SSD_STATE, SSD_HEAD_DIM), F32)],
        compiler_params=_params("parallel", "arbitrary"),
        name="ssd_mixer",
    )(proj, proj, dt_raw, conv_w.astype(F32), conv_b.astype(F32).reshape(1, cw), pad(dt_bias),
      pad(a_log), dskip, norm_w.astype(F32).reshape(1, SSD_WIDTH))


def _moba_kernel(q_ref, k_ref, v_ref, o_ref, kaug, *, seq):
    nb = seq // MOBA_BLOCK
    nbp = -(-nb // 8) * 8
    blk = MOBA_BLOCK
    n_sel = min(MOBA_TOPK, nb)
    nt = (((1,), (1,)), ((), ()))
    c_exp = (HEAD_DIM ** -0.5) * LOG2E

    kaug[:, :HEAD_DIM] = k_ref[...]
    row_blk = lax.broadcasted_iota(jnp.int32, (seq, LANES), 0) // blk
    col = lax.broadcasted_iota(jnp.int32, (seq, LANES), 1)
    kaug[:, HEAD_DIM:] = jnp.where(row_blk == col, 1.0, 0.0).astype(BF16)

    kmeans = [jnp.mean(k_ref[n * blk:(n + 1) * blk, :].astype(F32), axis=0, keepdims=True)
              for n in range(nb)]
    if nbp > nb:
        kmeans.append(jnp.zeros((nbp - nb, HEAD_DIM), F32))
    kmean = jnp.concatenate(kmeans, axis=0)
    sub = lax.broadcasted_iota(jnp.int32, (nbp, blk), 0)
    qi = lax.broadcasted_iota(jnp.int32, (blk, blk), 0)
    kj = lax.broadcasted_iota(jnp.int32, (blk, blk), 1)
    causal = kj <= qi

    for j in range(nb):
        qb = q_ref[j * blk:(j + 1) * blk, :]
        s_self = lax.dot_general(qb, k_ref[j * blk:(j + 1) * blk, :], nt, preferred_element_type=F32)
        s_self = jnp.where(causal, s_self, NEG)
        m = jnp.max(s_self, axis=-1, keepdims=True)
        if j > 0:
            gate_t = lax.dot_general(kmean, qb.astype(F32), nt, preferred_element_type=F32,
                                     precision=lax.Precision.HIGHEST)
            gate_t = jnp.where(sub < j, gate_t, NEG)
            bias_t = jnp.zeros((nbp, blk), F32)
            for n in range(j):
                gn = gate_t[n:n + 1, :]
                beats = (gate_t > gn) | ((gate_t == gn) & (sub < n))
                rank = jnp.sum(beats.astype(F32), axis=0, keepdims=True)
                bias_t = jnp.where(sub == n, jnp.where(rank < n_sel, 0.0, NEG), bias_t)
            bias_t = jnp.concatenate([bias_t, jnp.zeros((LANES - nbp, blk), F32)], axis=0)
            q_aug = jnp.concatenate([qb, bias_t.T.astype(BF16)], axis=1)
            s_past = lax.dot_general(q_aug, kaug[:j * blk, :], nt, preferred_element_type=F32)
            m = jnp.maximum(m, jnp.max(s_past, axis=-1, keepdims=True))
        p = jnp.exp2((s_self - m) * c_exp)
        den = jnp.sum(p, axis=-1, keepdims=True)
        acc = jnp.dot(p.astype(BF16), v_ref[j * blk:(j + 1) * blk, :], preferred_element_type=F32)
        if j > 0:
            p = jnp.exp2((s_past - m) * c_exp)
            den = den + jnp.sum(p, axis=-1, keepdims=True)
            acc = acc + jnp.dot(p.astype(BF16), v_ref[:j * blk, :], preferred_element_type=F32)
        o_ref[j * blk:(j + 1) * blk, :] = (acc / den).astype(o_ref.dtype)


def moba_attention(qkv, *, bsz, seq):
    blk = lambda off: pl.BlockSpec((None, seq, HEAD_DIM), lambda b, h, off=off: (b, 0, h + off))
    return pl.pallas_call(
        functools.partial(_moba_kernel, seq=seq),
        out_shape=jax.ShapeDtypeStruct((bsz, seq, C_HEADS * HEAD_DIM), BF16),
        grid=(bsz, C_HEADS),
        in_specs=[blk(0), blk(C_HEADS), blk(2 * C_HEADS)],
        out_specs=pl.BlockSpec((None, seq, HEAD_DIM), lambda b, h: (b, 0, h)),
        scratch_shapes=[pltpu.VMEM((seq, 2 * HEAD_DIM), BF16)],
        compiler_params=_params("parallel", "parallel"),
        name="moba_attention",
    )(qkv, qkv, qkv)


def _router_kernel(x_ref, w_ref, b_ref, idx_ref, gate_ref):
    logits = jnp.dot(x_ref[...], w_ref[...], preferred_element_type=F32,
                     precision=lax.Precision.HIGHEST) + b_ref[...]
    tm = logits.shape[0]
    lane = lax.broadcasted_iota(jnp.int32, (tm, LANES), 1)
    logits = jnp.where(lane < N_EXPERTS, logits, NEG)
    v1 = jnp.max(logits, axis=-1, keepdims=True)
    i1 = jnp.min(jnp.where(logits == v1, lane, LANES), axis=-1, keepdims=True)
    rest = jnp.where(lane == i1, NEG, logits)
    v2 = jnp.max(rest, axis=-1, keepdims=True)
    i2 = jnp.min(jnp.where(rest == v2, lane, LANES), axis=-1, keepdims=True)
    e2 = jnp.exp(v2 - v1)
    g1 = 1.0 / (1.0 + e2)
    g2 = e2 / (1.0 + e2)
    idx_ref[...] = jnp.where(lane == 0, i1, jnp.where(lane == 1, i2, 0))
    gate_ref[...] = jnp.where(lane == 0, g1, jnp.where(lane == 1, g2, 0.0))


def moe_router(x, w_router, b_router, *, tm):
    t, d = x.shape
    w = jnp.pad(w_router.astype(F32), ((0, 0), (0, LANES - N_EXPERTS)))
    b = jnp.pad(b_router.astype(F32), (0, LANES - N_EXPERTS)).reshape(1, LANES)
    return pl.pallas_call(
        _router_kernel,
        out_shape=(jax.ShapeDtypeStruct((t, LANES), jnp.int32), jax.ShapeDtypeStruct((t, LANES), F32)),
        grid=(t // tm,),
        in_specs=[pl.BlockSpec((tm, d), lambda i: (i, 0)),
                  pl.BlockSpec((d, LANES), lambda i: (0, 0)),
                  pl.BlockSpec((1, LANES), lambda i: (0, 0))],
        out_specs=(pl.BlockSpec((tm, LANES), lambda i: (i, 0)),
                   pl.BlockSpec((tm, LANES), lambda i: (i, 0))),
        compiler_params=_params("parallel"),
        name="moe_router",
    )(x, w, b)


def _gather_kernel(src0_ref, src1_ref, x_hbm, o_ref, buf, sems, *, rows):
    i = pl.program_id(0)

    def issue(src_ref, slot):
        def body(r, c):
            pltpu.make_async_copy(x_hbm.at[pl.ds(src_ref[0, 0, r], 1)],
                                  buf.at[slot, pl.ds(r, 1)], sems.at[slot]).start()
            return c
        lax.fori_loop(0, rows, body, 0, unroll=8)

    @pl.when(i == 0)
    def _():
        issue(src0_ref, 0)

    @pl.when(i + 1 < pl.num_programs(0))
    def _():
        issue(src1_ref, (i + 1) % 2)

    slot = i % 2
    pltpu.make_async_copy(x_hbm.at[pl.ds(0, rows)], buf.at[slot], sems.at[slot]).wait()
    o_ref[...] = buf[slot].astype(o_ref.dtype)


def gather_rows(x, src, *, rows):
    n = src.shape[0]
    d = x.shape[1]
    nblk = n // rows
    src3 = src.reshape(nblk, 1, rows)
    return pl.pallas_call(
        functools.partial(_gather_kernel, rows=rows),
        out_shape=jax.ShapeDtypeStruct((n, d), BF16),
        grid=(nblk,),
        in_specs=[pl.BlockSpec((1, 1, rows), lambda i: (0, 0, 0), memory_space=pltpu.SMEM),
                  pl.BlockSpec((1, 1, rows), lambda i: (jnp.minimum(i + 1, nblk - 1), 0, 0),
                               memory_space=pltpu.SMEM),
                  pl.BlockSpec(memory_space=pl.ANY)],
        out_specs=pl.BlockSpec((rows, d), lambda i: (i, 0)),
        scratch_shapes=[pltpu.VMEM((2, rows, d), F32), pltpu.SemaphoreType.DMA((2,))],
        compiler_params=_params("arbitrary"),
        name="moe_gather",
    )(src3, src3, x)


def _expert_changed(be_ref):
    i = pl.program_id(1)
    return (i == 0) | (be_ref[i] != be_ref[jnp.maximum(i - 1, 0)])


def _grouped_swiglu_kernel(be_ref, nu_ref, a_ref, wg_ref, wu_ref, o_ref, wgb_ref, wub_ref):
    @pl.when(_expert_changed(be_ref))
    def _():
        wgb_ref[...] = wg_ref[...].astype(BF16)
        wub_ref[...] = wu_ref[...].astype(BF16)

    @pl.when(pl.program_id(1) < nu_ref[0])
    def _():
        a = a_ref[...]
        g = jnp.dot(a, wgb_ref[...], preferred_element_type=F32)
        u = jnp.dot(a, wub_ref[...], preferred_element_type=F32)
        o_ref[...] = (_silu(g) * u).astype(o_ref.dtype)

    @pl.when(pl.program_id(1) >= nu_ref[0])
    def _():
        o_ref[...] = jnp.zeros_like(o_ref)


def grouped_swiglu(a, w_gu, layer, block_expert, n_used, *, tm, tn):
    n, k = a.shape
    f = w_gu.shape[3] // 2
    nf = f // tn
    grid_spec = pltpu.PrefetchScalarGridSpec(
        num_scalar_prefetch=2,
        grid=(nf, n // tm),
        in_specs=[pl.BlockSpec((tm, k), lambda j, i, be, nu: (i, 0)),
                  pl.BlockSpec((None, None, k, tn), lambda j, i, be, nu: (layer, be[i], 0, j)),
                  pl.BlockSpec((None, None, k, tn), lambda j, i, be, nu: (layer, be[i], 0, j + nf))],
        out_specs=pl.BlockSpec((tm, tn), lambda j, i, be, nu: (i, j)),
        scratch_shapes=[pltpu.VMEM((k, tn), BF16)] * 2,
    )
    return pl.pallas_call(
        _grouped_swiglu_kernel,
        out_shape=jax.ShapeDtypeStruct((n, f), BF16),
        grid_spec=grid_spec,
        compiler_params=_params("parallel", "arbitrary"),
        name="moe_grouped_swiglu",
    )(block_expert, n_used, a, w_gu, w_gu)


def _grouped_down_kernel(be_ref, nu_ref, a_ref, w_ref, o_ref, wb_ref):
    @pl.when(_expert_changed(be_ref))
    def _():
        wb_ref[...] = w_ref[...].astype(BF16)

    @pl.when(pl.program_id(1) < nu_ref[0])
    def _():
        o_ref[...] = jnp.dot(a_ref[...], wb_ref[...], preferred_element_type=F32)

    @pl.when(pl.program_id(1) >= nu_ref[0])
    def _():
        o_ref[...] = jnp.zeros_like(o_ref)


def grouped_down(a, w_down, layer, block_expert, n_used, *, tm, tn):
    n, f = a.shape
    d = w_down.shape[3]
    grid_spec = pltpu.PrefetchScalarGridSpec(
        num_scalar_prefetch=2,
        grid=(d // tn, n // tm),
        in_specs=[pl.BlockSpec((tm, f), lambda j, i, be, nu: (i, 0)),
                  pl.BlockSpec((None, None, f, tn), lambda j, i, be, nu: (layer, be[i], 0, j))],
        out_specs=pl.BlockSpec((tm, tn), lambda j, i, be, nu: (i, j)),
        scratch_shapes=[pltpu.VMEM((f, tn), BF16)],
    )
    return pl.pallas_call(
        _grouped_down_kernel,
        out_shape=jax.ShapeDtypeStruct((n, d), F32),
        grid_spec=grid_spec,
        compiler_params=_params("parallel", "arbitrary"),
        name="moe_grouped_down",
    )(block_expert, n_used, a, w_down)


def _combine_kernel(slot0_ref, slot1_ref, y_hbm, gate_ref, x_ref, g_ref, b_ref, o_ref, ob_ref, buf, sems,
                    *, rows, alpha):
    i = pl.program_id(0)

    def issue(slot_ref, slot):
        def body(r, c):
            for k in range(2):
                pltpu.make_async_copy(y_hbm.at[pl.ds(slot_ref[0, k, r], 1)],
                                      buf.at[slot, pl.ds(k * rows + r, 1)], sems.at[slot]).start()
            return c
        lax.fori_loop(0, rows, body, 0, unroll=4)

    @pl.when(i == 0)
    def _():
        issue(slot0_ref, 0)

    @pl.when(i + 1 < pl.num_programs(0))
    def _():
        issue(slot1_ref, (i + 1) % 2)

    slot = i % 2
    pltpu.make_async_copy(y_hbm.at[pl.ds(0, 2 * rows)], buf.at[slot], sems.at[slot]).wait()
    gates = gate_ref[...]
    h = gates[:, 0:1] * buf[slot, :rows, :] + gates[:, 1:2] * buf[slot, rows:, :]
    y = _layer_norm(alpha * x_ref[...] + h, g_ref[...], b_ref[...])
    o_ref[...] = y
    ob_ref[...] = y.astype(BF16)


def moe_combine_ln(y, slots, gates, x, g, b, *, rows, alpha):
    t, d = x.shape
    nblk = t // rows
    slots3 = slots.reshape(2, nblk, rows).transpose(1, 0, 2)
    return pl.pallas_call(
        functools.partial(_combine_kernel, rows=rows, alpha=alpha),
        out_shape=(jax.ShapeDtypeStruct((t, d), F32), jax.ShapeDtypeStruct((t, d), BF16)),
        grid=(nblk,),
        in_specs=[pl.BlockSpec((1, 2, rows), lambda i: (0, 0, 0), memory_space=pltpu.SMEM),
                  pl.BlockSpec((1, 2, rows), lambda i: (jnp.minimum(i + 1, nblk - 1), 0, 0),
                               memory_space=pltpu.SMEM),
                  pl.BlockSpec(memory_space=pl.ANY),
                  pl.BlockSpec((rows, LANES), lambda i: (i, 0)),
                  pl.BlockSpec((rows, d), lambda i: (i, 0)),
                  pl.BlockSpec((1, d), lambda i: (0, 0)),
                  pl.BlockSpec((1, d), lambda i: (0, 0))],
        out_specs=(pl.BlockSpec((rows, d), lambda i: (i, 0)),
                   pl.BlockSpec((rows, d), lambda i: (i, 0))),
        scratch_shapes=[pltpu.VMEM((2, 2 * rows, d), F32), pltpu.SemaphoreType.DMA((2,))],
        compiler_params=_params("arbitrary"),
        name="moe_combine_ln",
    )(slots3, slots3, y, gates, x, g.reshape(1, d), b.reshape(1, d))


def moe_dispatch_tables(top_idx, *, tm):
    t = top_idx.shape[0]
    n_rows = _padded_rows(t, tm)
    flat = top_idx.reshape(-1)
    onehot = (flat[:, None] == jnp.arange(N_EXPERTS)[None, :]).astype(jnp.int32)
    csum = jnp.cumsum(onehot, axis=0)
    rank = jnp.sum(csum * onehot, axis=1) - 1
    counts = csum[-1]
    padded = ((counts + tm - 1) // tm) * tm
    ends = jnp.cumsum(padded)
    starts = ends - padded
    dest = starts[flat] + rank
    src = jnp.zeros((n_rows,), jnp.int32).at[dest].set(jnp.arange(2 * t, dtype=jnp.int32) // 2)
    blk_start = jnp.arange(n_rows // tm, dtype=jnp.int32) * tm
    block_expert = jnp.minimum(jnp.sum(blk_start[:, None] >= ends[None, :], axis=1),
                               N_EXPERTS - 1).astype(jnp.int32)
    n_used = (ends[-1] // tm).astype(jnp.int32).reshape(1)
    slots = dest.reshape(t, 2).T.astype(jnp.int32)
    return src, slots, block_expert, n_used


def _padded_rows(t, tm):
    return 2 * t + N_EXPERTS * tm


def moe_layer(x, w_router, b_router, w_gu, w_down, layer, g, b, *, alpha):
    tm = 512
    idx, gates = moe_router(x, w_router, b_router, tm=512)
    src, slots, block_expert, n_used = moe_dispatch_tables(idx[:, :2], tm=tm)
    xs = gather_rows(x, src, rows=256)
    h = grouped_swiglu(xs, w_gu, layer, block_expert, n_used, tm=tm, tn=512)
    y = grouped_down(h, w_down, layer, block_expert, n_used, tm=tm, tn=512)
    return moe_combine_ln(y, slots, gates, x, g, b, rows=256, alpha=alpha)


def kernel(x, ev_w_in, ev_conv_w, ev_conv_b, ev_dt_bias, ev_a_log, ev_d_skip, ev_ssd_norm, ev_w_out, ev_ln1_g, ev_ln1_b, ev_ffn_w_gu, ev_ffn_w_down, ev_ln2_g, ev_ln2_b, od_w_qkv, od_w_out, od_ln1_g, od_ln1_b, od_router_w, od_router_b, od_exp_w_gu, od_exp_w_down, od_ln2_g, od_ln2_b):
    bsz, seq, d = x.shape
    t = bsz * seq
    depth = ev_w_in.shape[0] + od_w_qkv.shape[0]
    alpha = (2 * depth) ** 0.25
    qkvz = 3 * A_WIDTH + SSD_WIDTH + SSD_WIDTH + 2 * SSD_GROUPS * SSD_STATE

    xf = x.reshape(t, d).astype(F32)
    xb = xf.astype(BF16)
    for layer in range(depth):
        i = layer // 2
        if layer % 2 == 0:
            proj = matmul(xb, ev_w_in, i, n_out=qkvz, tm=1024, tn=1024, out_dtype=BF16)
            w_dt = jnp.pad(ev_w_in[i, :, qkvz:], ((0, 0), (0, LANES - SSD_HEADS)))[None]
            dt_raw = matmul(xb, w_dt, 0, n_out=LANES, tm=2048, tn=LANES, out_dtype=F32)
            proj = proj.reshape(bsz, seq, qkvz)
            y_a = dilated_attention(proj, bsz=bsz, seq=seq)
            y_b = ssd_mixer(proj, dt_raw.reshape(bsz, seq, LANES), ev_conv_w[i], ev_conv_b[i],
                            ev_dt_bias[i], ev_a_log[i], ev_d_skip[i], ev_ssd_norm[i],
                            bsz=bsz, seq=seq)
            mix = jnp.concatenate([y_a, y_b], axis=-1).reshape(t, A_WIDTH + SSD_WIDTH)
            xf, xb = matmul_res_ln(mix, ev_w_out[i].astype(BF16), xf, ev_ln1_g[i], ev_ln1_b[i],
                                   tm=512, tk=mix.shape[1], alpha=alpha)
            hid = matmul_swiglu(xb, ev_ffn_w_gu, i, tm=1024, tn=512)
            xf, xb = matmul_res_ln(hid, ev_ffn_w_down[i].astype(BF16), xf, ev_ln2_g[i], ev_ln2_b[i],
                                   tm=512, tk=1408, alpha=alpha)
        else:
            qkv = matmul(xb, od_w_qkv, i, n_out=3 * C_HEADS * HEAD_DIM, tm=1024, tn=1024, out_dtype=BF16)
            att = moba_attention(qkv.reshape(bsz, seq, 3 * C_HEADS * HEAD_DIM), bsz=bsz, seq=seq)
            xf, xb = matmul_res_ln(att.reshape(t, C_HEADS * HEAD_DIM), od_w_out[i].astype(BF16), xf,
                                   od_ln1_g[i], od_ln1_b[i], tm=512, tk=C_HEADS * HEAD_DIM, alpha=alpha)
            xf, xb = moe_layer(xf, od_router_w[i], od_router_b[i], od_exp_w_gu, od_exp_w_down, i,
                               od_ln2_g[i], od_ln2_b[i], alpha=alpha)
    return xf.reshape(bsz, seq, d).astype(x.dtype)
```

```python
import functools

import jax
import jax.numpy as jnp
from jax import lax
from jax.experimental import pallas as pl
from jax.experimental.pallas import tpu as pltpu

F32 = jnp.float32
BF16 = jnp.bfloat16

HEAD_DIM = 128
A_HEADS = 8
A_WIDTH = A_HEADS * HEAD_DIM
A_BRANCHES = ((128, 1), (512, 4), (2048, 16))
SSD_HEADS = 16
SSD_HEAD_DIM = 64
SSD_WIDTH = SSD_HEADS * SSD_HEAD_DIM
SSD_GROUPS = 4
SSD_STATE = 128
SSD_CONV = 4
SSD_CHUNK = 128
C_HEADS = 16
MOBA_BLOCK = 256
MOBA_TOPK = 3
N_EXPERTS = 8
LN_EPS = 1e-5
RMS_EPS = 1e-5
NEG = -1e30
LOG2E = 1.4426950408889634

LANES = 128
VMEM_LIMIT = 56 * 1024 * 1024


def _params(*sem):
    return pltpu.CompilerParams(dimension_semantics=sem, vmem_limit_bytes=VMEM_LIMIT)


def _silu(x):
    return x * (1.0 / (1.0 + jnp.exp(-x)))


def _layer_norm(z, g, b):
    mu = jnp.mean(z, axis=-1, keepdims=True)
    zc = z - mu
    var = jnp.mean(zc * zc, axis=-1, keepdims=True)
    return zc * lax.rsqrt(var + LN_EPS) * g + b


def _mm_kernel(a_ref, w_ref, o_ref, wb_ref):
    @pl.when(pl.program_id(1) == 0)
    def _():
        wb_ref[...] = w_ref[...].astype(BF16)

    o_ref[...] = jnp.dot(a_ref[...], wb_ref[...], preferred_element_type=F32).astype(o_ref.dtype)


def matmul(a, w, layer, *, n_out, tm, tn, out_dtype):
    m, k = a.shape
    return pl.pallas_call(
        _mm_kernel,
        out_shape=jax.ShapeDtypeStruct((m, n_out), out_dtype),
        grid=(n_out // tn, m // tm),
        in_specs=[pl.BlockSpec((tm, k), lambda j, i: (i, 0)),
                  pl.BlockSpec((None, k, tn), lambda j, i: (layer, 0, j))],
        out_specs=pl.BlockSpec((tm, tn), lambda j, i: (i, j)),
        scratch_shapes=[pltpu.VMEM((k, tn), BF16)],
        compiler_params=_params("parallel", "arbitrary"),
        name="matmul",
    )(a, w)


def _swiglu_kernel(a_ref, wg_ref, wu_ref, o_ref, wgb_ref, wub_ref):
    @pl.when(pl.program_id(1) == 0)
    def _():
        wgb_ref[...] = wg_ref[...].astype(BF16)
        wub_ref[...] = wu_ref[...].astype(BF16)

    a = a_ref[...]
    g = jnp.dot(a, wgb_ref[...], preferred_element_type=F32)
    u = jnp.dot(a, wub_ref[...], preferred_element_type=F32)
    o_ref[...] = (_silu(g) * u).astype(o_ref.dtype)


def matmul_swiglu(a, w_gu, layer, *, tm, tn):
    m, k = a.shape
    f = w_gu.shape[2] // 2
    nf = f // tn
    return pl.pallas_call(
        _swiglu_kernel,
        out_shape=jax.ShapeDtypeStruct((m, f), BF16),
        grid=(nf, m // tm),
        in_specs=[pl.BlockSpec((tm, k), lambda j, i: (i, 0)),
                  pl.BlockSpec((None, k, tn), lambda j, i: (layer, 0, j)),
                  pl.BlockSpec((None, k, tn), lambda j, i: (layer, 0, j + nf))],
        out_specs=pl.BlockSpec((tm, tn), lambda j, i: (i, j)),
        scratch_shapes=[pltpu.VMEM((k, tn), BF16)] * 2,
        compiler_params=_params("parallel", "arbitrary"),
        name="matmul_swiglu",
    )(a, w_gu, w_gu)


def _res_ln_kernel(a_ref, w_ref, x_ref, g_ref, b_ref, o_ref, ob_ref, acc_ref, *, nk, alpha):
    k = pl.program_id(1)

    @pl.when(k == 0)
    def _():
        acc_ref[...] = alpha * x_ref[...]

    acc_ref[...] += jnp.dot(a_ref[...], w_ref[...], preferred_element_type=F32)

    @pl.when(k == nk - 1)
    def _():
        y = _layer_norm(acc_ref[...], g_ref[...], b_ref[...])
        o_ref[...] = y
        ob_ref[...] = y.astype(BF16)


def matmul_res_ln(a, w, x, g, b, *, tm, tk, alpha):
    m, k = a.shape
    d = w.shape[1]
    nk = k // tk
    return pl.pallas_call(
        functools.partial(_res_ln_kernel, nk=nk, alpha=alpha),
        out_shape=(jax.ShapeDtypeStruct((m, d), F32), jax.ShapeDtypeStruct((m, d), BF16)),
        grid=(m // tm, nk),
        in_specs=[pl.BlockSpec((tm, tk), lambda i, kk: (i, kk)),
                  pl.BlockSpec((tk, d), lambda i, kk: (kk, 0)),
                  pl.BlockSpec((tm, d), lambda i, kk: (i, 0)),
                  pl.BlockSpec((1, d), lambda i, kk: (0, 0)),
                  pl.BlockSpec((1, d), lambda i, kk: (0, 0))],
        out_specs=(pl.BlockSpec((tm, d), lambda i, kk: (i, 0)),
                   pl.BlockSpec((tm, d), lambda i, kk: (i, 0))),
        scratch_shapes=[pltpu.VMEM((tm, d), F32)],
        compiler_params=_params("parallel", "arbitrary"),
        name="matmul_res_ln",
    )(a, w, x, g.reshape(1, d), b.reshape(1, d))


def _dilated_kernel(q_ref, k_ref, v_ref, o_ref, qf, kf, vf, qg, kg, vg, of, lf, mf, *, seq):
    (w0, d0), (w1, g), (w2, d2) = A_BRANCHES
    assert d0 == 1 and d2 == g * g and w0 // d0 == w1 // g == w2 // d2
    band = w0
    sub = seq // g
    assert seq // d2 == band
    scale = HEAD_DIM ** -0.5
    nt = (((1,), (1,)), ((), ()))

    qf[...] = q_ref[...].astype(F32)
    kf[...] = k_ref[...].astype(F32)
    vf[...] = v_ref[...].astype(F32)
    for r in range(g):
        nat = pl.ds(r, sub, stride=g)
        grp = pl.ds(r * sub, sub)
        qg[grp, :] = qf[nat, :]
        kg[grp, :] = kf[nat, :]
        vg[grp, :] = vf[nat, :]

    qi = lax.broadcasted_iota(jnp.int32, (band, 2 * band), 0)
    kj = lax.broadcasted_iota(jnp.int32, (band, 2 * band), 1)
    dist = band + qi - kj
    in_win2 = (dist >= 0) & (dist <= band)
    causal = (lax.broadcasted_iota(jnp.int32, (band, band), 1)
              <= lax.broadcasted_iota(jnp.int32, (band, band), 0))

    def attend(qb, kb, vb, mask):
        s = lax.dot_general(qb.astype(BF16), kb.astype(BF16), nt, preferred_element_type=F32) * scale
        s = jnp.where(mask, s, NEG)
        m = jnp.max(s, axis=-1, keepdims=True)
        p = jnp.exp(s - m)
        den = jnp.sum(p, axis=-1, keepdims=True)
        o = jnp.dot(p.astype(BF16), vb.astype(BF16), preferred_element_type=F32) / den
        return o, jnp.broadcast_to(m + jnp.log(den), (band, HEAD_DIM))

    def banded(bi, qs, ks, vs, n_blk):
        for idx in range(seq // band):
            rows = pl.ds(idx * band, band)
            if idx % n_blk == 0:
                o, lse = attend(qs[rows, :], ks[rows, :], vs[rows, :], causal)
            else:
                both = pl.ds((idx - 1) * band, 2 * band)
                o, lse = attend(qs[rows, :], ks[both, :], vs[both, :], in_win2)
            of[bi, rows, :] = o
            lf[bi, rows, :] = lse

    banded(0, qf, kf, vf, seq // band)
    banded(1, qg, kg, vg, sub // band)
    for r in range(g):
        for m in range(g):
            rows = pl.ds(r * sub + m, band, stride=g)
            o, lse = attend(qg[rows, :], kg[rows, :], vg[rows, :], causal)
            of[2, rows, :] = o
            lf[2, rows, :] = lse

    for r in range(g):
        nat = pl.ds(r, sub, stride=g)
        grp = pl.ds(r * sub, sub)
        l0, l1, l2 = lf[0, nat, :], lf[1, grp, :], lf[2, grp, :]
        mx = jnp.maximum(jnp.maximum(l0, l1), l2)
        e0, e1, e2 = jnp.exp(l0 - mx), jnp.exp(l1 - mx), jnp.exp(l2 - mx)
        mf[nat, :] = (e0 * of[0, nat, :] + e1 * of[1, grp, :] + e2 * of[2, grp, :]) / (e0 + e1 + e2)
    o_ref[...] = mf[...].astype(o_ref.dtype)


def dilated_attention(proj, *, bsz, seq):
    kern = functools.partial(_dilated_kernel, seq=seq)
    blk = lambda off: pl.BlockSpec((None, seq, HEAD_DIM), lambda b, h, off=off: (b, 0, h + off))
    return pl.pallas_call(
        kern,
        out_shape=jax.ShapeDtypeStruct((bsz, seq, A_WIDTH), BF16),
        grid=(bsz, A_HEADS),
        in_specs=[blk(0), blk(A_HEADS), blk(2 * A_HEADS)],
        out_specs=pl.BlockSpec((None, seq, HEAD_DIM), lambda b, h: (b, 0, h)),
        scratch_shapes=[pltpu.VMEM((seq, HEAD_DIM), F32)] * 6
                       + [pltpu.VMEM((len(A_BRANCHES), seq, HEAD_DIM), F32)] * 2
                       + [pltpu.VMEM((seq, HEAD_DIM), F32)],
        compiler_params=_params("parallel", "parallel"),
        name="dilated_attention",
    )(proj, proj, proj)


def _ssd_kernel(z_ref, xbc_ref, dt_ref, cw_ref, cb_ref, dtb_ref, alog_ref, dskip_ref, nw_ref,
                o_ref, tail, hstate):
    q = SSD_CHUNK
    c = pl.program_id(1)

    @pl.when(c == 0)
    def _():
        tail[...] = jnp.zeros_like(tail)
        hstate[...] = jnp.zeros_like(hstate)

    x_cur = xbc_ref[...].astype(F32)
    xe = jnp.concatenate([tail[...], x_cur], axis=0)
    tail[...] = x_cur[q - 8:, :]
    conv = cb_ref[...]
    for j in range(SSD_CONV):
        off = 8 - (SSD_CONV - 1) + j
        conv = conv + cw_ref[j:j + 1, :] * xe[off:off + q, :]
    xbc = _silu(conv)
    gn = SSD_GROUPS * SSD_STATE
    xs = xbc[:, :SSD_WIDTH]
    bm = xbc[:, SSD_WIDTH:SSD_WIDTH + gn]
    cm = xbc[:, SSD_WIDTH + gn:]

    dt_in = dt_ref[...] + dtb_ref[...]
    dt = jnp.maximum(dt_in, 0.0) + jnp.log1p(jnp.exp(-jnp.abs(dt_in)))
    a_head = -jnp.exp(alog_ref[...])
    dta = dt * a_head
    li = lax.broadcasted_iota(jnp.int32, (q, q), 0)
    si = lax.broadcasted_iota(jnp.int32, (q, q), 1)
    causal = li >= si
    a_cs = jnp.dot(causal.astype(F32), dta, preferred_element_type=F32,
                   precision=lax.Precision.HIGHEST)
    a_cs_t = a_cs.T

    bm_b = bm.astype(BF16)
    cm_b = cm.astype(BF16)
    rep = SSD_HEADS // SSD_GROUPS
    ys = []
    for g in range(SSD_GROUPS):
        bg = bm_b[:, g * SSD_STATE:(g + 1) * SSD_STATE]
        cg = cm_b[:, g * SSD_STATE:(g + 1) * SSD_STATE]
        cbg = lax.dot_general(cg, bg, (((1,), (1,)), ((), ())), preferred_element_type=F32)
        bg_t = bm[:, g * SSD_STATE:(g + 1) * SSD_STATE].T
        for r in range(rep):
            h = g * rep + r
            col = a_cs[:, h:h + 1]
            row = a_cs_t[h:h + 1, :]
            decay_in = jnp.exp(jnp.where(causal, col - row, NEG))
            xh = xs[:, h * SSD_HEAD_DIM:(h + 1) * SSD_HEAD_DIM]
            xdt = (xh * dt[:, h:h + 1]).astype(BF16)
            y = jnp.dot((cbg * decay_in).astype(BF16), xdt, preferred_element_type=F32)
            h_prev = hstate[h]
            y_off = jnp.dot(cg, h_prev.astype(BF16), preferred_element_type=F32) * jnp.exp(col)
            last = a_cs_t[h:h + 1, q - 1:q]
            decay_st = jnp.exp(last - row)
            st = jnp.dot((bg_t * decay_st).astype(BF16), xdt, preferred_element_type=F32)
            hstate[h] = jnp.exp(last) * h_prev + st
            ys.append(y + y_off)
    y = jnp.concatenate(ys, axis=-1) + dskip_ref[...] * xs
    y = y * _silu(z_ref[...].astype(F32))
    gw = SSD_WIDTH // SSD_GROUPS
    outs = []
    for g in range(SSD_GROUPS):
        yg = y[:, g * gw:(g + 1) * gw]
        outs.append(yg * lax.rsqrt(jnp.mean(yg * yg, axis=-1, keepdims=True) + RMS_EPS))
    o_ref[...] = (jnp.concatenate(outs, axis=-1) * nw_ref[...]).astype(o_ref.dtype)


def ssd_mixer(proj, dt_raw, conv_w, conv_b, dt_bias, a_log, d_skip, norm_w, *, bsz, seq):
    cw = SSD_WIDTH + 2 * SSD_GROUPS * SSD_STATE
    pad = lambda v: jnp.pad(v.astype(F32), (0, LANES - SSD_HEADS)).reshape(1, LANES)
    dskip = jnp.repeat(d_skip.astype(F32), SSD_HEAD_DIM).reshape(1, SSD_WIDTH)
    const = lambda shape: pl.BlockSpec(shape, lambda b, c: (0, 0))
    z_blk = (3 * A_WIDTH) // SSD_WIDTH
    xbc_blk = (3 * A_WIDTH + SSD_WIDTH) // cw
    return pl.pallas_call(
        _ssd_kernel,
        out_shape=jax.ShapeDtypeStruct((bsz, seq, SSD_WIDTH), BF16),
        grid=(bsz, seq // SSD_CHUNK),
        in_specs=[pl.BlockSpec((None, SSD_CHUNK, SSD_WIDTH), lambda b, c: (b, c, z_blk)),
                  pl.BlockSpec((None, SSD_CHUNK, cw), lambda b, c: (b, c, xbc_blk)),
                  pl.BlockSpec((None, SSD_CHUNK, LANES), lambda b, c: (b, c, 0)),
                  const((SSD_CONV, cw)), const((1, cw)), const((1, LANES)), const((1, LANES)),
                  const((1, SSD_WIDTH)), const((1, SSD_WIDTH))],
        out_specs=pl.BlockSpec((None, SSD_CHUNK, SSD_WIDTH), lambda b, c: (b, c, 0)),
        scratch_shapes=[pltpu.VMEM((8, cw), F32),
                        pltpu.VMEM((SSD_HEADS, SSD_STATE, SSD_HEAD_DIM), F32)],
        compiler_params=_params("parallel", "arbitrary"),
        name="ssd_mixer",
    )(proj, proj, dt_raw, conv_w.astype(F32), conv_b.astype(F32).reshape(1, cw), pad(dt_bias),
      pad(a_log), dskip, norm_w.astype(F32).reshape(1, SSD_WIDTH))


def _moba_kernel(q_ref, k_ref, v_ref, o_ref, kaug, qaug, *, seq):
    nb = seq // MOBA_BLOCK
    nbp = -(-nb // 8) * 8
    blk = MOBA_BLOCK
    n_sel = min(MOBA_TOPK, nb)
    nt = (((1,), (1,)), ((), ()))
    c_exp = (HEAD_DIM ** -0.5) * LOG2E

    kaug[:, :HEAD_DIM] = k_ref[...]
    row_blk = lax.broadcasted_iota(jnp.int32, (seq, LANES), 0) // blk
    col = lax.broadcasted_iota(jnp.int32, (seq, LANES), 1)
    kaug[:, HEAD_DIM:] = jnp.where(row_blk == col, 1.0, 0.0).astype(BF16)

    kmeans = [jnp.mean(k_ref[n * blk:(n + 1) * blk, :].astype(F32), axis=0, keepdims=True)
              for n in range(nb)]
    if nbp > nb:
        kmeans.append(jnp.zeros((nbp - nb, HEAD_DIM), F32))
    kmean = jnp.concatenate(kmeans, axis=0)

    sub = lax.broadcasted_iota(jnp.int32, (nbp, seq), 0)
    q_blk = lax.broadcasted_iota(jnp.int32, (nbp, seq), 1) // blk
    gate_t = lax.dot_general(kmean, q_ref[...].astype(F32), nt, preferred_element_type=F32,
                             precision=lax.Precision.HIGHEST)
    gate_t = jnp.where(sub < q_blk, gate_t, NEG)
    bias_t = jnp.zeros((nbp, seq), F32)
    for n in range(nb - 1):
        gn = gate_t[n:n + 1, :]
        beats = (gate_t > gn) | ((gate_t == gn) & (sub < n))
        rank = jnp.sum(beats.astype(F32), axis=0, keepdims=True)
        bias_t = jnp.where(sub == n, jnp.where(rank < n_sel, 0.0, NEG), bias_t)
    qaug[:, :HEAD_DIM] = q_ref[...]
    pad = jnp.zeros((LANES - nbp, blk), F32)
    for j in range(nb):
        bj = jnp.concatenate([bias_t[:, j * blk:(j + 1) * blk], pad], axis=0)
        qaug[j * blk:(j + 1) * blk, HEAD_DIM:] = bj.T.astype(BF16)

    qi = lax.broadcasted_iota(jnp.int32, (blk, blk), 0)
    kj = lax.broadcasted_iota(jnp.int32, (blk, blk), 1)
    causal = kj <= qi

    for j in range(nb):
        qb = q_ref[j * blk:(j + 1) * blk, :]
        s_self = lax.dot_general(qb, k_ref[j * blk:(j + 1) * blk, :], nt, preferred_element_type=F32)
        s_self = jnp.where(causal, s_self, NEG)
        m = jnp.max(s_self, axis=-1, keepdims=True)
        if j > 0:
            s_past = lax.dot_general(qaug[j * blk:(j + 1) * blk, :], kaug[:j * blk, :], nt,
                                     preferred_element_type=F32)
            m = jnp.maximum(m, jnp.max(s_past, axis=-1, keepdims=True))
        p = jnp.exp2((s_self - m) * c_exp)
        den = jnp.sum(p, axis=-1, keepdims=True)
        acc = jnp.dot(p.astype(BF16), v_ref[j * blk:(j + 1) * blk, :], preferred_element_type=F32)
        if j > 0:
            p = jnp.exp2((s_past - m) * c_exp)
            den = den + jnp.sum(p, axis=-1, keepdims=True)
            acc = acc + jnp.dot(p.astype(BF16), v_ref[:j * blk, :], preferred_element_type=F32)
        o_ref[j * blk:(j + 1) * blk, :] = (acc / den).astype(o_ref.dtype)


def moba_attention(qkv, *, bsz, seq):
    blk = lambda off: pl.BlockSpec((None, seq, HEAD_DIM), lambda b, h, off=off: (b, 0, h + off))
    return pl.pallas_call(
        functools.partial(_moba_kernel, seq=seq),
        out_shape=jax.ShapeDtypeStruct((bsz, seq, C_HEADS * HEAD_DIM), BF16),
        grid=(bsz, C_HEADS),
        in_specs=[blk(0), blk(C_HEADS), blk(2 * C_HEADS)],
        out_specs=pl.BlockSpec((None, seq, HEAD_DIM), lambda b, h: (b, 0, h)),
        scratch_shapes=[pltpu.VMEM((seq, 2 * HEAD_DIM), BF16)] * 2,
        compiler_params=_params("parallel", "parallel"),
        name="moba_attention",
    )(qkv, qkv, qkv)


def _router_kernel(x_ref, w_ref, b_ref, idx_ref, gate_ref):
    logits = jnp.dot(x_ref[...], w_ref[...], preferred_element_type=F32,
                     precision=lax.Precision.HIGHEST) + b_ref[...]
    tm = logits.shape[0]
    lane = lax.broadcasted_iota(jnp.int32, (tm, LANES), 1)
    logits = jnp.where(lane < N_EXPERTS, logits, NEG)
    v1 = jnp.max(logits, axis=-1, keepdims=True)
    i1 = jnp.min(jnp.where(logits == v1, lane, LANES), axis=-1, keepdims=True)
    rest = jnp.where(lane == i1, NEG, logits)
    v2 = jnp.max(rest, axis=-1, keepdims=True)
    i2 = jnp.min(jnp.where(rest == v2, lane, LANES), axis=-1, keepdims=True)
    e2 = jnp.exp(v2 - v1)
    g1 = 1.0 / (1.0 + e2)
    g2 = e2 / (1.0 + e2)
    idx_ref[...] = jnp.where(lane == 0, i1, jnp.where(lane == 1, i2, 0))
    gate_ref[...] = jnp.where(lane == 0, g1, jnp.where(lane == 1, g2, 0.0))


def moe_router(x, w_router, b_router, *, tm):
    t, d = x.shape
    w = jnp.pad(w_router.astype(F32), ((0, 0), (0, LANES - N_EXPERTS)))
    b = jnp.pad(b_router.astype(F32), (0, LANES - N_EXPERTS)).reshape(1, LANES)
    return pl.pallas_call(
        _router_kernel,
        out_shape=(jax.ShapeDtypeStruct((t, LANES), jnp.int32), jax.ShapeDtypeStruct((t, LANES), F32)),
        grid=(t // tm,),
        in_specs=[pl.BlockSpec((tm, d), lambda i: (i, 0)),
                  pl.BlockSpec((d, LANES), lambda i: (0, 0)),
                  pl.BlockSpec((1, LANES), lambda i: (0, 0))],
        out_specs=(pl.BlockSpec((tm, LANES), lambda i: (i, 0)),
                   pl.BlockSpec((tm, LANES), lambda i: (i, 0))),
        compiler_params=_params("parallel"),
        name="moe_router",
    )(x, w, b)


def _gather_kernel(src0_ref, src1_ref, x_hbm, o_ref, buf, sems, *, rows):
    i = pl.program_id(0)

    def issue(src_ref, slot):
        def body(r, c):
            pltpu.make_async_copy(x_hbm.at[pl.ds(src_ref[0, 0, r], 1)],
                                  buf.at[slot, pl.ds(r, 1)], sems.at[slot]).start()
            return c
        lax.fori_loop(0, rows, body, 0, unroll=8)

    @pl.when(i == 0)
    def _():
        issue(src0_ref, 0)

    @pl.when(i + 1 < pl.num_programs(0))
    def _():
        issue(src1_ref, (i + 1) % 2)

    slot = i % 2
    pltpu.make_async_copy(x_hbm.at[pl.ds(0, rows)], buf.at[slot], sems.at[slot]).wait()
    o_ref[...] = buf[slot].astype(o_ref.dtype)


def gather_rows(x, src, *, rows):
    n = src.shape[0]
    d = x.shape[1]
    nblk = n // rows
    src3 = src.reshape(nblk, 1, rows)
    return pl.pallas_call(
        functools.partial(_gather_kernel, rows=rows),
        out_shape=jax.ShapeDtypeStruct((n, d), BF16),
        grid=(nblk,),
        in_specs=[pl.BlockSpec((1, 1, rows), lambda i: (0, 0, 0), memory_space=pltpu.SMEM),
                  pl.BlockSpec((1, 1, rows), lambda i: (jnp.minimum(i + 1, nblk - 1), 0, 0),
                               memory_space=pltpu.SMEM),
                  pl.BlockSpec(memory_space=pl.ANY)],
        out_specs=pl.BlockSpec((rows, d), lambda i: (i, 0)),
        scratch_shapes=[pltpu.VMEM((2, rows, d), F32), pltpu.SemaphoreType.DMA((2,))],
        compiler_params=_params("arbitrary"),
        name="moe_gather",
    )(src3, src3, x)


def _expert_changed(be_ref):
    i = pl.program_id(1)
    return (i == 0) | (be_ref[i] != be_ref[jnp.maximum(i - 1, 0)])


def _grouped_kernel(be_ref, nxt_ref, nu_ref, a_ref, w_hbm, o_ref, wf_ref, wb_ref, sems, seg_ref,
                    *, layer, tn, n_col, swiglu):
    j, i = pl.program_id(0), pl.program_id(1)
    halves = 2 if swiglu else 1

    def tile_copy(jj, e, slot, half):
        col = pl.multiple_of((jj + half * n_col) * tn, tn)
        return pltpu.make_async_copy(w_hbm.at[layer, e, :, pl.ds(col, tn)], wf_ref.at[slot, half],
                                     sems.at[slot, half])

    def fetch(jj, e, slot):
        for half in range(halves):
            tile_copy(jj, e, slot, half).start()

    @pl.when((j == 0) & (i == 0))
    def _():
        seg_ref[0] = 0
        fetch(0, be_ref[0], 0)

    @pl.when(_expert_changed(be_ref))
    def _():
        slot = seg_ref[0] % 2
        for half in range(halves):
            tile_copy(j, be_ref[i], slot, half).wait()
            wb_ref[half] = wf_ref[slot, half].astype(BF16)
        e_next = nxt_ref[i]

        @pl.when(e_next >= 0)
        def _():
            fetch(j, e_next, 1 - slot)

        @pl.when((e_next < 0) & (j + 1 < n_col))
        def _():
            fetch(j + 1, be_ref[0], 1 - slot)

        seg_ref[0] = seg_ref[0] + 1

    @pl.when(i < nu_ref[0])
    def _():
        a = a_ref[...]
        if swiglu:
            g = jnp.dot(a, wb_ref[0], preferred_element_type=F32)
            u = jnp.dot(a, wb_ref[1], preferred_element_type=F32)
            o_ref[...] = (_silu(g) * u).astype(o_ref.dtype)
        else:
            o_ref[...] = jnp.dot(a, wb_ref[0], preferred_element_type=F32).astype(o_ref.dtype)

    @pl.when(i >= nu_ref[0])
    def _():
        o_ref[...] = jnp.zeros_like(o_ref)


def _grouped_call(a, w, layer, tables, *, n_out, tm, tn, swiglu, out_dtype, name):
    block_expert, next_expert, n_used = tables
    n, k = a.shape
    n_col = n_out // tn
    halves = 2 if swiglu else 1
    grid_spec = pltpu.PrefetchScalarGridSpec(
        num_scalar_prefetch=3,
        grid=(n_col, n // tm),
        in_specs=[pl.BlockSpec((tm, k), lambda j, i, be, nx, nu: (i, 0)),
                  pl.BlockSpec(memory_space=pl.ANY)],
        out_specs=pl.BlockSpec((tm, tn), lambda j, i, be, nx, nu: (i, j)),
        scratch_shapes=[pltpu.VMEM((2, halves, k, tn), F32), pltpu.VMEM((halves, k, tn), BF16),
                        pltpu.SemaphoreType.DMA((2, halves)), pltpu.SMEM((1,), jnp.int32)],
    )
    return pl.pallas_call(
        functools.partial(_grouped_kernel, layer=layer, tn=tn, n_col=n_col, swiglu=swiglu),
        out_shape=jax.ShapeDtypeStruct((n, n_out), out_dtype),
        grid_spec=grid_spec,
        compiler_params=_params("arbitrary", "arbitrary"),
        name=name,
    )(block_expert, next_expert, n_used, a, w)


def grouped_swiglu(a, w_gu, layer, tables, *, tm, tn):
    return _grouped_call(a, w_gu, layer, tables, n_out=w_gu.shape[3] // 2, tm=tm, tn=tn, swiglu=True,
                         out_dtype=BF16, name="moe_grouped_swiglu")


def grouped_down(a, w_down, layer, tables, *, tm, tn):
    return _grouped_call(a, w_down, layer, tables, n_out=w_down.shape[3], tm=tm, tn=tn, swiglu=False,
                         out_dtype=F32, name="moe_grouped_down")


def _combine_kernel(slot0_ref, slot1_ref, y_hbm, gate_ref, x_ref, g_ref, b_ref, o_ref, ob_ref, buf, sems,
                    *, rows, alpha):
    i = pl.program_id(0)

    def issue(slot_ref, slot):
        def body(r, c):
            for k in range(2):
                pltpu.make_async_copy(y_hbm.at[pl.ds(slot_ref[0, k, r], 1)],
                                      buf.at[slot, pl.ds(k * rows + r, 1)], sems.at[slot]).start()
            return c
        lax.fori_loop(0, rows, body, 0, unroll=4)

    @pl.when(i == 0)
    def _():
        issue(slot0_ref, 0)

    @pl.when(i + 1 < pl.num_programs(0))
    def _():
        issue(slot1_ref, (i + 1) % 2)

    slot = i % 2
    pltpu.make_async_copy(y_hbm.at[pl.ds(0, 2 * rows)], buf.at[slot], sems.at[slot]).wait()
    gates = gate_ref[...]
    h = gates[:, 0:1] * buf[slot, :rows, :] + gates[:, 1:2] * buf[slot, rows:, :]
    y = _layer_norm(alpha * x_ref[...] + h, g_ref[...], b_ref[...])
    o_ref[...] = y
    ob_ref[...] = y.astype(BF16)


def moe_combine_ln(y, slots, gates, x, g, b, *, rows, alpha):
    t, d = x.shape
    nblk = t // rows
    slots3 = slots.reshape(2, nblk, rows).transpose(1, 0, 2)
    return pl.pallas_call(
        functools.partial(_combine_kernel, rows=rows, alpha=alpha),
        out_shape=(jax.ShapeDtypeStruct((t, d), F32), jax.ShapeDtypeStruct((t, d), BF16)),
        grid=(nblk,),
        in_specs=[pl.BlockSpec((1, 2, rows), lambda i: (0, 0, 0), memory_space=pltpu.SMEM),
                  pl.BlockSpec((1, 2, rows), lambda i: (jnp.minimum(i + 1, nblk - 1), 0, 0),
                               memory_space=pltpu.SMEM),
                  pl.BlockSpec(memory_space=pl.ANY),
                  pl.BlockSpec((rows, LANES), lambda i: (i, 0)),
                  pl.BlockSpec((rows, d), lambda i: (i, 0)),
                  pl.BlockSpec((1, d), lambda i: (0, 0)),
                  pl.BlockSpec((1, d), lambda i: (0, 0))],
        out_specs=(pl.BlockSpec((rows, d), lambda i: (i, 0)),
                   pl.BlockSpec((rows, d), lambda i: (i, 0))),
        scratch_shapes=[pltpu.VMEM((2, 2 * rows, d), F32), pltpu.SemaphoreType.DMA((2,))],
        compiler_params=_params("arbitrary"),
        name="moe_combine_ln",
    )(slots3, slots3, y, gates, x, g.reshape(1, d), b.reshape(1, d))


def moe_dispatch_tables(top_idx, *, tm):
    t = top_idx.shape[0]
    n_rows = _padded_rows(t, tm)
    flat = top_idx.reshape(-1)
    onehot = (flat[:, None] == jnp.arange(N_EXPERTS)[None, :]).astype(jnp.int32)
    csum = jnp.cumsum(onehot, axis=0)
    rank = jnp.sum(csum * onehot, axis=1) - 1
    counts = csum[-1]
    padded = ((counts + tm - 1) // tm) * tm
    ends = jnp.cumsum(padded)
    starts = ends - padded
    dest = starts[flat] + rank
    src = jnp.zeros((n_rows,), jnp.int32).at[dest].set(jnp.arange(2 * t, dtype=jnp.int32) // 2)
    n_blk = n_rows // tm
    n_used = (ends[-1] // tm).astype(jnp.int32)
    blk_start = jnp.minimum(jnp.arange(n_blk, dtype=jnp.int32), n_used - 1) * tm
    block_expert = jnp.sum(blk_start[:, None] >= ends[None, :], axis=1).astype(jnp.int32)
    nxt_idx = jnp.searchsorted(block_expert, block_expert, side='right')
    next_expert = jnp.where(nxt_idx < n_blk, block_expert[jnp.minimum(nxt_idx, n_blk - 1)], -1)
    slots = dest.reshape(t, 2).T.astype(jnp.int32)
    return src, slots, (block_expert, next_expert.astype(jnp.int32), n_used.reshape(1))


def _padded_rows(t, tm):
    return 2 * t + N_EXPERTS * tm


def moe_layer(x, w_router, b_router, w_gu, w_down, layer, g, b, *, alpha):
    tm = 512
    idx, gates = moe_router(x, w_router, b_router, tm=512)
    src, slots, tables = moe_dispatch_tables(idx[:, :2], tm=tm)
    xs = gather_rows(x, src, rows=256)
    h = grouped_swiglu(xs, w_gu, layer, tables, tm=tm, tn=512)
    y = grouped_down(h, w_down, layer, tables, tm=tm, tn=512)
    return moe_combine_ln(y, slots, gates, x, g, b, rows=256, alpha=alpha)


def kernel(x, ev_w_in, ev_conv_w, ev_conv_b, ev_dt_bias, ev_a_log, ev_d_skip, ev_ssd_norm, ev_w_out, ev_ln1_g, ev_ln1_b, ev_ffn_w_gu, ev_ffn_w_down, ev_ln2_g, ev_ln2_b, od_w_qkv, od_w_out, od_ln1_g, od_ln1_b, od_router_w, od_router_b, od_exp_w_gu, od_exp_w_down, od_ln2_g, od_ln2_b):
    bsz, seq, d = x.shape
    t = bsz * seq
    depth = ev_w_in.shape[0] + od_w_qkv.shape[0]
    alpha = (2 * depth) ** 0.25
    qkvz = 3 * A_WIDTH + SSD_WIDTH + SSD_WIDTH + 2 * SSD_GROUPS * SSD_STATE

    xf = x.reshape(t, d).astype(F32)
    xb = xf.astype(BF16)
    for layer in range(depth):
        i = layer // 2
        if layer % 2 == 0:
            proj = matmul(xb, ev_w_in, i, n_out=qkvz, tm=1024, tn=1024, out_dtype=BF16)
            w_dt = jnp.pad(ev_w_in[i, :, qkvz:], ((0, 0), (0, LANES - SSD_HEADS)))[None]
            dt_raw = matmul(xb, w_dt, 0, n_out=LANES, tm=2048, tn=LANES, out_dtype=F32)
            proj = proj.reshape(bsz, seq, qkvz)
            y_a = dilated_attention(proj, bsz=bsz, seq=seq)
            y_b = ssd_mixer(proj, dt_raw.reshape(bsz, seq, LANES), ev_conv_w[i], ev_conv_b[i],
                            ev_dt_bias[i], ev_a_log[i], ev_d_skip[i], ev_ssd_norm[i],
                            bsz=bsz, seq=seq)
            mix = jnp.concatenate([y_a, y_b], axis=-1).reshape(t, A_WIDTH + SSD_WIDTH)
            xf, xb = matmul_res_ln(mix, ev_w_out[i].astype(BF16), xf, ev_ln1_g[i], ev_ln1_b[i],
                                   tm=512, tk=mix.shape[1], alpha=alpha)
            hid = matmul_swiglu(xb, ev_ffn_w_gu, i, tm=1024, tn=512)
            xf, xb = matmul_res_ln(hid, ev_ffn_w_down[i].astype(BF16), xf, ev_ln2_g[i], ev_ln2_b[i],
                                   tm=512, tk=1408, alpha=alpha)
        else:
            qkv = matmul(xb, od_w_qkv, i, n_out=3 * C_HEADS * HEAD_DIM, tm=1024, tn=1024, out_dtype=BF16)
            att = moba_attention(qkv.reshape(bsz, seq, 3 * C_HEADS * HEAD_DIM), bsz=bsz, seq=seq)
            xf, xb = matmul_res_ln(att.reshape(t, C_HEADS * HEAD_DIM), od_w_out[i].astype(BF16), xf,
                                   od_ln1_g[i], od_ln1_b[i], tm=512, tk=C_HEADS * HEAD_DIM, alpha=alpha)
            xf, xb = moe_layer(xf, od_router_w[i], od_router_b[i], od_exp_w_gu, od_exp_w_down, i,
                               od_ln2_g[i], od_ln2_b[i], alpha=alpha)
    return xf.reshape(bsz, seq, d).astype(x.dtype)
```

```python
import functools

import jax
import jax.numpy as jnp
from jax import lax
from jax.experimental import pallas as pl
from jax.experimental.pallas import tpu as pltpu

F32 = jnp.float32
BF16 = jnp.bfloat16

HEAD_DIM = 128
A_HEADS = 8
A_WIDTH = A_HEADS * HEAD_DIM
A_BRANCHES = ((128, 1), (512, 4), (2048, 16))
SSD_HEADS = 16
SSD_HEAD_DIM = 64
SSD_WIDTH = SSD_HEADS * SSD_HEAD_DIM
SSD_GROUPS = 4
SSD_STATE = 128
SSD_CONV = 4
SSD_CHUNK = 128
C_HEADS = 16
MOBA_BLOCK = 256
MOBA_TOPK = 3
N_EXPERTS = 8
LN_EPS = 1e-5
RMS_EPS = 1e-5
NEG = -1e30
LOG2E = 1.4426950408889634

LANES = 128
VMEM_LIMIT = 56 * 1024 * 1024


def _params(*sem):
    return pltpu.CompilerParams(dimension_semantics=sem, vmem_limit_bytes=VMEM_LIMIT)


def _silu(x):
    return x * (1.0 / (1.0 + jnp.exp(-x)))


def _layer_norm(z, g, b):
    mu = jnp.mean(z, axis=-1, keepdims=True)
    zc = z - mu
    var = jnp.mean(zc * zc, axis=-1, keepdims=True)
    return zc * lax.rsqrt(var + LN_EPS) * g + b


def _mm_kernel(a_ref, w_ref, o_ref, wb_ref):
    @pl.when(pl.program_id(1) == 0)
    def _():
        wb_ref[...] = w_ref[...].astype(BF16)

    o_ref[...] = jnp.dot(a_ref[...], wb_ref[...], preferred_element_type=F32).astype(o_ref.dtype)


def matmul(a, w, layer, *, n_out, tm, tn, out_dtype):
    m, k = a.shape
    return pl.pallas_call(
        _mm_kernel,
        out_shape=jax.ShapeDtypeStruct((m, n_out), out_dtype),
        grid=(n_out // tn, m // tm),
        in_specs=[pl.BlockSpec((tm, k), lambda j, i: (i, 0)),
                  pl.BlockSpec((None, k, tn), lambda j, i: (layer, 0, j))],
        out_specs=pl.BlockSpec((tm, tn), lambda j, i: (i, j)),
        scratch_shapes=[pltpu.VMEM((k, tn), BF16)],
        compiler_params=_params("parallel", "arbitrary"),
        name="matmul",
    )(a, w)


def _swiglu_kernel(a_ref, wg_ref, wu_ref, o_ref, wgb_ref, wub_ref):
    @pl.when(pl.program_id(1) == 0)
    def _():
        wgb_ref[...] = wg_ref[...].astype(BF16)
        wub_ref[...] = wu_ref[...].astype(BF16)

    a = a_ref[...]
    g = jnp.dot(a, wgb_ref[...], preferred_element_type=F32)
    u = jnp.dot(a, wub_ref[...], preferred_element_type=F32)
    o_ref[...] = (_silu(g) * u).astype(o_ref.dtype)


def matmul_swiglu(a, w_gu, layer, *, tm, tn):
    m, k = a.shape
    f = w_gu.shape[2] // 2
    nf = f // tn
    return pl.pallas_call(
        _swiglu_kernel,
        out_shape=jax.ShapeDtypeStruct((m, f), BF16),
        grid=(nf, m // tm),
        in_specs=[pl.BlockSpec((tm, k), lambda j, i: (i, 0)),
                  pl.BlockSpec((None, k, tn), lambda j, i: (layer, 0, j)),
                  pl.BlockSpec((None, k, tn), lambda j, i: (layer, 0, j + nf))],
        out_specs=pl.BlockSpec((tm, tn), lambda j, i: (i, j)),
        scratch_shapes=[pltpu.VMEM((k, tn), BF16)] * 2,
        compiler_params=_params("parallel", "arbitrary"),
        name="matmul_swiglu",
    )(a, w_gu, w_gu)


def _res_ln_kernel(*refs, widths, alpha):
    n = len(widths)
    a_refs, (w_ref, x_ref, g_ref, b_ref, o_ref, ob_ref) = refs[:n], refs[n:]
    z = alpha * x_ref[...]
    k0 = 0
    for a_ref, kw in zip(a_refs, widths):
        z = z + jnp.dot(a_ref[...], w_ref[k0:k0 + kw, :], preferred_element_type=F32)
        k0 += kw
    y = _layer_norm(z, g_ref[...], b_ref[...])
    o_ref[...] = y
    ob_ref[...] = y.astype(BF16)


def matmul_res_ln(a_parts, w, x, g, b, *, tm, alpha):
    m = x.shape[0]
    k, d = w.shape
    widths = tuple(a.shape[1] for a in a_parts)
    assert sum(widths) == k
    row = lambda width: pl.BlockSpec((tm, width), lambda i: (i, 0))
    const = lambda shape, **kw: pl.BlockSpec(shape, lambda i: (0, 0), **kw)
    return pl.pallas_call(
        functools.partial(_res_ln_kernel, widths=widths, alpha=alpha),
        out_shape=(jax.ShapeDtypeStruct((m, d), F32), jax.ShapeDtypeStruct((m, d), BF16)),
        grid=(m // tm,),
        in_specs=[row(width) for width in widths]
                 + [const((k, d), pipeline_mode=pl.Buffered(1)), row(d), const((1, d)), const((1, d))],
        out_specs=(row(d), row(d)),
        compiler_params=_params("parallel"),
        name="matmul_res_ln",
    )(*a_parts, w, x, g.reshape(1, d), b.reshape(1, d))


def _dilated_kernel(q_ref, k_ref, v_ref, o_ref, qf, kf, vf, qg, kg, vg, of, lf, mf, *, seq):
    (w0, d0), (w1, g), (w2, d2) = A_BRANCHES
    assert d0 == 1 and d2 == g * g and w0 // d0 == w1 // g == w2 // d2
    band = w0
    sub = seq // g
    assert seq // d2 == band
    scale = HEAD_DIM ** -0.5
    nt = (((1,), (1,)), ((), ()))

    qf[...] = q_ref[...].astype(F32)
    kf[...] = k_ref[...].astype(F32)
    vf[...] = v_ref[...].astype(F32)
    for r in range(g):
        nat = pl.ds(r, sub, stride=g)
        grp = pl.ds(r * sub, sub)
        qg[grp, :] = qf[nat, :]
        kg[grp, :] = kf[nat, :]
        vg[grp, :] = vf[nat, :]

    qi = lax.broadcasted_iota(jnp.int32, (band, 2 * band), 0)
    kj = lax.broadcasted_iota(jnp.int32, (band, 2 * band), 1)
    dist = band + qi - kj
    in_win2 = (dist >= 0) & (dist <= band)
    causal = (lax.broadcasted_iota(jnp.int32, (band, band), 1)
              <= lax.broadcasted_iota(jnp.int32, (band, band), 0))

    def attend(qb, kb, vb, mask):
        s = lax.dot_general(qb.astype(BF16), kb.astype(BF16), nt, preferred_element_type=F32) * scale
        s = jnp.where(mask, s, NEG)
        m = jnp.max(s, axis=-1, keepdims=True)
        p = jnp.exp(s - m)
        den = jnp.sum(p, axis=-1, keepdims=True)
        o = jnp.dot(p.astype(BF16), vb.astype(BF16), preferred_element_type=F32) / den
        return o, jnp.broadcast_to(m + jnp.log(den), (band, HEAD_DIM))

    def banded(bi, qs, ks, vs, n_blk):
        for idx in range(seq // band):
            rows = pl.ds(idx * band, band)
            if idx % n_blk == 0:
                o, lse = attend(qs[rows, :], ks[rows, :], vs[rows, :], causal)
            else:
                both = pl.ds((idx - 1) * band, 2 * band)
                o, lse = attend(qs[rows, :], ks[both, :], vs[both, :], in_win2)
            of[bi, rows, :] = o
            lf[bi, rows, :] = lse

    banded(0, q_ref, k_ref, v_ref, seq // band)
    banded(1, qg, kg, vg, sub // band)
    for r in range(g):
        for m in range(g):
            rows = pl.ds(r * sub + m, band, stride=g)
            o, lse = attend(qg[rows, :], kg[rows, :], vg[rows, :], causal)
            of[2, rows, :] = o
            lf[2, rows, :] = lse

    for r in range(g):
        nat = pl.ds(r, sub, stride=g)
        grp = pl.ds(r * sub, sub)
        l0, l1, l2 = lf[0, nat, :], lf[1, grp, :], lf[2, grp, :]
        mx = jnp.maximum(jnp.maximum(l0, l1), l2)
        e0, e1, e2 = jnp.exp(l0 - mx), jnp.exp(l1 - mx), jnp.exp(l2 - mx)
        mf[nat, :] = (e0 * of[0, nat, :] + e1 * of[1, grp, :] + e2 * of[2, grp, :]) / (e0 + e1 + e2)
    o_ref[...] = mf[...].astype(o_ref.dtype)


def dilated_attention(proj, *, bsz, seq):
    kern = functools.partial(_dilated_kernel, seq=seq)
    blk = lambda off: pl.BlockSpec((None, seq, HEAD_DIM), lambda b, h, off=off: (b, 0, h + off))
    return pl.pallas_call(
        kern,
        out_shape=jax.ShapeDtypeStruct((bsz, seq, A_WIDTH), BF16),
        grid=(bsz, A_HEADS),
        in_specs=[blk(0), blk(A_HEADS), blk(2 * A_HEADS)],
        out_specs=pl.BlockSpec((None, seq, HEAD_DIM), lambda b, h: (b, 0, h)),
        scratch_shapes=[pltpu.VMEM((seq, HEAD_DIM), F32)] * 6
                       + [pltpu.VMEM((len(A_BRANCHES), seq, HEAD_DIM), F32)] * 2
                       + [pltpu.VMEM((seq, HEAD_DIM), F32)],
        compiler_params=_params("parallel", "parallel"),
        name="dilated_attention",
    )(proj, proj, proj)


def _ssd_kernel(z_ref, xbc_ref, dt_ref, cw_ref, cb_ref, dtb_ref, alog_ref, dskip_ref, nw_ref,
                o_ref, tail, hstate):
    q = SSD_CHUNK
    c = pl.program_id(1)

    @pl.when(c == 0)
    def _():
        tail[...] = jnp.zeros_like(tail)
        hstate[...] = jnp.zeros_like(hstate)

    x_cur = xbc_ref[...].astype(F32)
    xe = jnp.concatenate([tail[...], x_cur], axis=0)
    tail[...] = x_cur[q - 8:, :]
    conv = cb_ref[...]
    for j in range(SSD_CONV):
        off = 8 - (SSD_CONV - 1) + j
        conv = conv + cw_ref[j:j + 1, :] * xe[off:off + q, :]
    xbc = _silu(conv)
    gn = SSD_GROUPS * SSD_STATE
    xs = xbc[:, :SSD_WIDTH]
    bm = xbc[:, SSD_WIDTH:SSD_WIDTH + gn]
    cm = xbc[:, SSD_WIDTH + gn:]

    dt_in = dt_ref[...] + dtb_ref[...]
    dt = jnp.maximum(dt_in, 0.0) + jnp.log1p(jnp.exp(-jnp.abs(dt_in)))
    a_head = -jnp.exp(alog_ref[...])
    dta = dt * a_head
    li = lax.broadcasted_iota(jnp.int32, (q, q), 0)
    si = lax.broadcasted_iota(jnp.int32, (q, q), 1)
    causal = li >= si
    a_cs = jnp.dot(causal.astype(F32), dta, preferred_element_type=F32,
                   precision=lax.Precision.HIGHEST)
    a_cs_t = a_cs.T

    bm_b = bm.astype(BF16)
    cm_b = cm.astype(BF16)
    rep = SSD_HEADS // SSD_GROUPS
    ys = []
    for g in range(SSD_GROUPS):
        bg = bm_b[:, g * SSD_STATE:(g + 1) * SSD_STATE]
        cg = cm_b[:, g * SSD_STATE:(g + 1) * SSD_STATE]
        cbg = lax.dot_general(cg, bg, (((1,), (1,)), ((), ())), preferred_element_type=F32)
        bg_t = bm[:, g * SSD_STATE:(g + 1) * SSD_STATE].T
        for r in range(rep):
            h = g * rep + r
            col = a_cs[:, h:h + 1]
            row = a_cs_t[h:h + 1, :]
            decay_in = jnp.exp(jnp.where(causal, col - row, NEG))
            xh = xs[:, h * SSD_HEAD_DIM:(h + 1) * SSD_HEAD_DIM]
            xdt = (xh * dt[:, h:h + 1]).astype(BF16)
            y = jnp.dot((cbg * decay_in).astype(BF16), xdt, preferred_element_type=F32)
            h_prev = hstate[h]
            y_off = jnp.dot(cg, h_prev.astype(BF16), preferred_element_type=F32) * jnp.exp(col)
            last = a_cs_t[h:h + 1, q - 1:q]
            decay_st = jnp.exp(last - row)
            st = jnp.dot((bg_t * decay_st).astype(BF16), xdt, preferred_element_type=F32)
            hstate[h] = jnp.exp(last) * h_prev + st
            ys.append(y + y_off)
    y = jnp.concatenate(ys, axis=-1) + dskip_ref[...] * xs
    y = y * _silu(z_ref[...].astype(F32))
    gw = SSD_WIDTH // SSD_GROUPS
    outs = []
    for g in range(SSD_GROUPS):
        yg = y[:, g * gw:(g + 1) * gw]
        outs.append(yg * lax.rsqrt(jnp.mean(yg * yg, axis=-1, keepdims=True) + RMS_EPS))
    o_ref[...] = (jnp.concatenate(outs, axis=-1) * nw_ref[...]).astype(o_ref.dtype)


def ssd_mixer(proj, dt_raw, conv_w, conv_b, dt_bias, a_log, d_skip, norm_w, *, bsz, seq):
    cw = SSD_WIDTH + 2 * SSD_GROUPS * SSD_STATE
    pad = lambda v: jnp.pad(v.astype(F32), (0, LANES - SSD_HEADS)).reshape(1, LANES)
    dskip = jnp.repeat(d_skip.astype(F32), SSD_HEAD_DIM).reshape(1, SSD_WIDTH)
    const = lambda shape: pl.BlockSpec(shape, lambda b, c: (0, 0))
    z_blk = (3 * A_WIDTH) // SSD_WIDTH
    xbc_blk = (3 * A_WIDTH + SSD_WIDTH) // cw
    return pl.pallas_call(
        _ssd_kernel,
        out_shape=jax.ShapeDtypeStruct((bsz, seq, SSD_WIDTH), BF16),
        grid=(bsz, seq // SSD_CHUNK),
        in_specs=[pl.BlockSpec((None, SSD_CHUNK, SSD_WIDTH), lambda b, c: (b, c, z_blk)),
                  pl.BlockSpec((None, SSD_CHUNK, cw), lambda b, c: (b, c, xbc_blk)),
                  pl.BlockSpec((None, SSD_CHUNK, LANES), lambda b, c: (b, c, 0)),
                  const((SSD_CONV, cw)), const((1, cw)), const((1, LANES)), const((1, LANES)),
                  const((1, SSD_WIDTH)), const((1, SSD_WIDTH))],
        out_specs=pl.BlockSpec((None, SSD_CHUNK, SSD_WIDTH), lambda b, c: (b, c, 0)),
        scratch_shapes=[pltpu.VMEM((8, cw), F32),
                        pltpu.VMEM((SSD_HEADS, SSD_STATE, SSD_HEAD_DIM), F32)],
        compiler_params=_params("parallel", "arbitrary"),
        name="ssd_mixer",
    )(proj, proj, dt_raw, conv_w.astype(F32), conv_b.astype(F32).reshape(1, cw), pad(dt_bias),
      pad(a_log), dskip, norm_w.astype(F32).reshape(1, SSD_WIDTH))


def _moba_kernel(q_ref, k_ref, v_ref, o_ref, kaug, qaug, *, seq):
    nb = seq // MOBA_BLOCK
    nbp = -(-nb // 8) * 8
    blk = MOBA_BLOCK
    n_sel = min(MOBA_TOPK, nb)
    nt = (((1,), (1,)), ((), ()))
    c_exp = (HEAD_DIM ** -0.5) * LOG2E

    kaug[:, :HEAD_DIM] = k_ref[...]
    row_blk = lax.broadcasted_iota(jnp.int32, (seq, LANES), 0) // blk
    col = lax.broadcasted_iota(jnp.int32, (seq, LANES), 1)
    kaug[:, HEAD_DIM:] = jnp.where(row_blk == col, 1.0, 0.0).astype(BF16)

    kmeans = [jnp.mean(k_ref[n * blk:(n + 1) * blk, :].astype(F32), axis=0, keepdims=True)
              for n in range(nb)]
    if nbp > nb:
        kmeans.append(jnp.zeros((nbp - nb, HEAD_DIM), F32))
    kmean = jnp.concatenate(kmeans, axis=0)

    sub = lax.broadcasted_iota(jnp.int32, (nbp, seq), 0)
    q_blk = lax.broadcasted_iota(jnp.int32, (nbp, seq), 1) // blk
    gate_t = lax.dot_general(kmean, q_ref[...].astype(F32), nt, preferred_element_type=F32,
                             precision=lax.Precision.HIGHEST)
    gate_t = jnp.where(sub < q_blk, gate_t, NEG)
    bias_t = jnp.zeros((nbp, seq), F32)
    for n in range(nb - 1):
        gn = gate_t[n:n + 1, :]
        beats = (gate_t > gn) | ((gate_t == gn) & (sub < n))
        rank = jnp.sum(beats.astype(F32), axis=0, keepdims=True)
        bias_t = jnp.where(sub == n, jnp.where(rank < n_sel, 0.0, NEG), bias_t)
    qaug[:, :HEAD_DIM] = q_ref[...]
    pad = jnp.zeros((LANES - nbp, blk), F32)
    for j in range(nb):
        bj = jnp.concatenate([bias_t[:, j * blk:(j + 1) * blk], pad], axis=0)
        qaug[j * blk:(j + 1) * blk, HEAD_DIM:] = bj.T.astype(BF16)

    qi = lax.broadcasted_iota(jnp.int32, (blk, blk), 0)
    kj = lax.broadcasted_iota(jnp.int32, (blk, blk), 1)
    causal = kj <= qi

    for j in range(nb):
        qb = q_ref[j * blk:(j + 1) * blk, :]
        s_self = lax.dot_general(qb, k_ref[j * blk:(j + 1) * blk, :], nt, preferred_element_type=F32)
        s_self = jnp.where(causal, s_self, NEG)
        m = jnp.max(s_self, axis=-1, keepdims=True)
        if j > 0:
            s_past = lax.dot_general(qaug[j * blk:(j + 1) * blk, :], kaug[:j * blk, :], nt,
                                     preferred_element_type=F32)
            m = jnp.maximum(m, jnp.max(s_past, axis=-1, keepdims=True))
        p = jnp.exp2((s_self - m) * c_exp)
        den = jnp.sum(p, axis=-1, keepdims=True)
        acc = jnp.dot(p.astype(BF16), v_ref[j * blk:(j + 1) * blk, :], preferred_element_type=F32)
        if j > 0:
            p = jnp.exp2((s_past - m) * c_exp)
            den = den + jnp.sum(p, axis=-1, keepdims=True)
            acc = acc + jnp.dot(p.astype(BF16), v_ref[:j * blk, :], preferred_element_type=F32)
        o_ref[j * blk:(j + 1) * blk, :] = (acc / den).astype(o_ref.dtype)


def moba_attention(qkv, *, bsz, seq):
    blk = lambda off: pl.BlockSpec((None, seq, HEAD_DIM), lambda b, h, off=off: (b, 0, h + off))
    return pl.pallas_call(
        functools.partial(_moba_kernel, seq=seq),
        out_shape=jax.ShapeDtypeStruct((bsz, seq, C_HEADS * HEAD_DIM), BF16),
        grid=(bsz, C_HEADS),
        in_specs=[blk(0), blk(C_HEADS), blk(2 * C_HEADS)],
        out_specs=pl.BlockSpec((None, seq, HEAD_DIM), lambda b, h: (b, 0, h)),
        scratch_shapes=[pltpu.VMEM((seq, 2 * HEAD_DIM), BF16)] * 2,
        compiler_params=_params("parallel", "parallel"),
        name="moba_attention",
    )(qkv, qkv, qkv)


def _router_kernel(x_ref, w_ref, b_ref, idx_ref, gate_ref):
    logits = jnp.dot(x_ref[...], w_ref[...], preferred_element_type=F32,
                     precision=lax.Precision.HIGHEST) + b_ref[...]
    tm = logits.shape[0]
    lane = lax.broadcasted_iota(jnp.int32, (tm, LANES), 1)
    logits = jnp.where(lane < N_EXPERTS, logits, NEG)
    v1 = jnp.max(logits, axis=-1, keepdims=True)
    i1 = jnp.min(jnp.where(logits == v1, lane, LANES), axis=-1, keepdims=True)
    rest = jnp.where(lane == i1, NEG, logits)
    v2 = jnp.max(rest, axis=-1, keepdims=True)
    i2 = jnp.min(jnp.where(rest == v2, lane, LANES), axis=-1, keepdims=True)
    e2 = jnp.exp(v2 - v1)
    g1 = 1.0 / (1.0 + e2)
    g2 = e2 / (1.0 + e2)
    idx_ref[...] = jnp.where(lane == 0, i1, jnp.where(lane == 1, i2, 0))
    gate_ref[...] = jnp.where(lane == 0, g1, jnp.where(lane == 1, g2, 0.0))


def moe_router(x, w_router, b_router, *, tm):
    t, d = x.shape
    w = jnp.pad(w_router.astype(F32), ((0, 0), (0, LANES - N_EXPERTS)))
    b = jnp.pad(b_router.astype(F32), (0, LANES - N_EXPERTS)).reshape(1, LANES)
    return pl.pallas_call(
        _router_kernel,
        out_shape=(jax.ShapeDtypeStruct((t, LANES), jnp.int32), jax.ShapeDtypeStruct((t, LANES), F32)),
        grid=(t // tm,),
        in_specs=[pl.BlockSpec((tm, d), lambda i: (i, 0)),
                  pl.BlockSpec((d, LANES), lambda i: (0, 0)),
                  pl.BlockSpec((1, LANES), lambda i: (0, 0))],
        out_specs=(pl.BlockSpec((tm, LANES), lambda i: (i, 0)),
                   pl.BlockSpec((tm, LANES), lambda i: (i, 0))),
        compiler_params=_params("parallel"),
        name="moe_router",
    )(x, w, b)


GATHER_AHEAD = 2


def _gather_kernel(head_ref, ahead_ref, x_hbm, o_ref, buf, sems, *, rows):
    i = pl.program_id(0)
    n_slot = GATHER_AHEAD + 1

    def issue(src_ref, first, slot):
        def body(r, c):
            pltpu.make_async_copy(x_hbm.at[pl.ds(src_ref[0, 0, first + r], 1)],
                                  buf.at[slot, pl.ds(r, 1)], sems.at[slot]).start()
            return c
        lax.fori_loop(0, rows, body, 0, unroll=8)

    @pl.when(i == 0)
    def _():
        for blk in range(GATHER_AHEAD):
            issue(head_ref, blk * rows, blk)

    @pl.when(i + GATHER_AHEAD < pl.num_programs(0))
    def _():
        issue(ahead_ref, 0, (i + GATHER_AHEAD) % n_slot)

    slot = i % n_slot
    pltpu.make_async_copy(x_hbm.at[pl.ds(0, rows)], buf.at[slot], sems.at[slot]).wait()
    o_ref[...] = buf[slot].astype(o_ref.dtype)


def gather_rows(x, src, *, rows):
    n = src.shape[0]
    d = x.shape[1]
    nblk = n // rows
    assert nblk > GATHER_AHEAD
    src3 = src.reshape(nblk, 1, rows)
    head = src[:GATHER_AHEAD * rows].reshape(1, 1, GATHER_AHEAD * rows)
    return pl.pallas_call(
        functools.partial(_gather_kernel, rows=rows),
        out_shape=jax.ShapeDtypeStruct((n, d), BF16),
        grid=(nblk,),
        in_specs=[pl.BlockSpec((1, 1, GATHER_AHEAD * rows), lambda i: (0, 0, 0), memory_space=pltpu.SMEM),
                  pl.BlockSpec((1, 1, rows), lambda i: (jnp.minimum(i + GATHER_AHEAD, nblk - 1), 0, 0),
                               memory_space=pltpu.SMEM),
                  pl.BlockSpec(memory_space=pl.ANY)],
        out_specs=pl.BlockSpec((rows, d), lambda i: (i, 0)),
        scratch_shapes=[pltpu.VMEM((GATHER_AHEAD + 1, rows, d), F32),
                        pltpu.SemaphoreType.DMA((GATHER_AHEAD + 1,))],
        compiler_params=_params("arbitrary"),
        name="moe_gather",
    )(head, src3, x)


def _expert_changed(be_ref):
    i = pl.program_id(1)
    return (i == 0) | (be_ref[i] != be_ref[jnp.maximum(i - 1, 0)])


def _grouped_kernel(be_ref, nxt_ref, nu_ref, a_ref, w_hbm, o_ref, wf_ref, wb_ref, sems, seg_ref,
                    *, layer, tn, n_col, swiglu):
    j, i = pl.program_id(0), pl.program_id(1)
    halves = 2 if swiglu else 1

    def tile_copy(jj, e, slot, half):
        col = pl.multiple_of((jj + half * n_col) * tn, tn)
        return pltpu.make_async_copy(w_hbm.at[layer, e, :, pl.ds(col, tn)], wf_ref.at[slot, half],
                                     sems.at[slot, half])

    def fetch(jj, e, slot):
        for half in range(halves):
            tile_copy(jj, e, slot, half).start()

    @pl.when((j == 0) & (i == 0))
    def _():
        seg_ref[0] = 0
        fetch(0, be_ref[0], 0)

    @pl.when(_expert_changed(be_ref))
    def _():
        slot = seg_ref[0] % 2
        for half in range(halves):
            tile_copy(j, be_ref[i], slot, half).wait()
            wb_ref[half] = wf_ref[slot, half].astype(BF16)
        e_next = nxt_ref[i]

        @pl.when(e_next >= 0)
        def _():
            fetch(j, e_next, 1 - slot)

        @pl.when((e_next < 0) & (j + 1 < n_col))
        def _():
            fetch(j + 1, be_ref[0], 1 - slot)

        seg_ref[0] = seg_ref[0] + 1

    @pl.when(i < nu_ref[0])
    def _():
        a = a_ref[...]
        if swiglu:
            g = jnp.dot(a, wb_ref[0], preferred_element_type=F32)
            u = jnp.dot(a, wb_ref[1], preferred_element_type=F32)
            o_ref[...] = (_silu(g) * u).astype(o_ref.dtype)
        else:
            o_ref[...] = jnp.dot(a, wb_ref[0], preferred_element_type=F32).astype(o_ref.dtype)

    @pl.when(i >= nu_ref[0])
    def _():
        o_ref[...] = jnp.zeros_like(o_ref)


def _grouped_call(a, w, layer, tables, *, n_out, tm, tn, swiglu, out_dtype, name):
    block_expert, next_expert, n_used = tables
    n, k = a.shape
    n_col = n_out // tn
    halves = 2 if swiglu else 1
    grid_spec = pltpu.PrefetchScalarGridSpec(
        num_scalar_prefetch=3,
        grid=(n_col, n // tm),
        in_specs=[pl.BlockSpec((tm, k), lambda j, i, be, nx, nu: (i, 0)),
                  pl.BlockSpec(memory_space=pl.ANY)],
        out_specs=pl.BlockSpec((tm, tn), lambda j, i, be, nx, nu: (i, j)),
        scratch_shapes=[pltpu.VMEM((2, halves, k, tn), F32), pltpu.VMEM((halves, k, tn), BF16),
                        pltpu.SemaphoreType.DMA((2, halves)), pltpu.SMEM((1,), jnp.int32)],
    )
    return pl.pallas_call(
        functools.partial(_grouped_kernel, layer=layer, tn=tn, n_col=n_col, swiglu=swiglu),
        out_shape=jax.ShapeDtypeStruct((n, n_out), out_dtype),
        grid_spec=grid_spec,
        compiler_params=_params("arbitrary", "arbitrary"),
        name=name,
    )(block_expert, next_expert, n_used, a, w)


def grouped_swiglu(a, w_gu, layer, tables, *, tm, tn):
    return _grouped_call(a, w_gu, layer, tables, n_out=w_gu.shape[3] // 2, tm=tm, tn=tn, swiglu=True,
                         out_dtype=BF16, name="moe_grouped_swiglu")


def grouped_down(a, w_down, layer, tables, *, tm, tn):
    return _grouped_call(a, w_down, layer, tables, n_out=w_down.shape[3], tm=tm, tn=tn, swiglu=False,
                         out_dtype=F32, name="moe_grouped_down")


def _combine_kernel(slot0_ref, slot1_ref, y_hbm, gate_ref, x_ref, g_ref, b_ref, o_ref, ob_ref, buf, sems,
                    *, rows, alpha):
    i = pl.program_id(0)

    def issue(slot_ref, slot):
        def body(r, c):
            for k in range(2):
                pltpu.make_async_copy(y_hbm.at[pl.ds(slot_ref[0, k, r], 1)],
                                      buf.at[slot, pl.ds(k * rows + r, 1)], sems.at[slot]).start()
            return c
        lax.fori_loop(0, rows, body, 0, unroll=4)

    @pl.when(i == 0)
    def _():
        issue(slot0_ref, 0)

    @pl.when(i + 1 < pl.num_programs(0))
    def _():
        issue(slot1_ref, (i + 1) % 2)

    slot = i % 2
    pltpu.make_async_copy(y_hbm.at[pl.ds(0, 2 * rows)], buf.at[slot], sems.at[slot]).wait()
    gates = gate_ref[...]
    h = gates[:, 0:1] * buf[slot, :rows, :] + gates[:, 1:2] * buf[slot, rows:, :]
    y = _layer_norm(alpha * x_ref[...] + h, g_ref[...], b_ref[...])
    o_ref[...] = y
    ob_ref[...] = y.astype(BF16)


def moe_combine_ln(y, slots, gates, x, g, b, *, rows, alpha):
    t, d = x.shape
    nblk = t // rows
    slots3 = slots.reshape(2, nblk, rows).transpose(1, 0, 2)
    return pl.pallas_call(
        functools.partial(_combine_kernel, rows=rows, alpha=alpha),
        out_shape=(jax.ShapeDtypeStruct((t, d), F32), jax.ShapeDtypeStruct((t, d), BF16)),
        grid=(nblk,),
        in_specs=[pl.BlockSpec((1, 2, rows), lambda i: (0, 0, 0), memory_space=pltpu.SMEM),
                  pl.BlockSpec((1, 2, rows), lambda i: (jnp.minimum(i + 1, nblk - 1), 0, 0),
                               memory_space=pltpu.SMEM),
                  pl.BlockSpec(memory_space=pl.ANY),
                  pl.BlockSpec((rows, LANES), lambda i: (i, 0)),
                  pl.BlockSpec((rows, d), lambda i: (i, 0)),
                  pl.BlockSpec((1, d), lambda i: (0, 0)),
                  pl.BlockSpec((1, d), lambda i: (0, 0))],
        out_specs=(pl.BlockSpec((rows, d), lambda i: (i, 0)),
                   pl.BlockSpec((rows, d), lambda i: (i, 0))),
        scratch_shapes=[pltpu.VMEM((2, 2 * rows, d), F32), pltpu.SemaphoreType.DMA((2,))],
        compiler_params=_params("arbitrary"),
        name="moe_combine_ln",
    )(slots3, slots3, y, gates, x, g.reshape(1, d), b.reshape(1, d))


def moe_dispatch_tables(top_idx, *, tm):
    t = top_idx.shape[0]
    n_rows = _padded_rows(t, tm)
    flat = top_idx.reshape(-1)
    onehot = (flat[:, None] == jnp.arange(N_EXPERTS)[None, :]).astype(jnp.int32)
    csum = jnp.cumsum(onehot, axis=0)
    rank = jnp.sum(csum * onehot, axis=1) - 1
    counts = csum[-1]
    padded = ((counts + tm - 1) // tm) * tm
    ends = jnp.cumsum(padded)
    starts = ends - padded
    dest = starts[flat] + rank
    src = jnp.zeros((n_rows,), jnp.int32).at[dest].set(jnp.arange(2 * t, dtype=jnp.int32) // 2)
    n_blk = n_rows // tm
    n_used = (ends[-1] // tm).astype(jnp.int32)
    blk_start = jnp.minimum(jnp.arange(n_blk, dtype=jnp.int32), n_used - 1) * tm
    block_expert = jnp.sum(blk_start[:, None] >= ends[None, :], axis=1).astype(jnp.int32)
    nxt_idx = jnp.searchsorted(block_expert, block_expert, side='right')
    next_expert = jnp.where(nxt_idx < n_blk, block_expert[jnp.minimum(nxt_idx, n_blk - 1)], -1)
    slots = dest.reshape(t, 2).T.astype(jnp.int32)
    return src, slots, (block_expert, next_expert.astype(jnp.int32), n_used.reshape(1))


def _padded_rows(t, tm):
    return 2 * t + N_EXPERTS * tm


def moe_layer(x, w_router, b_router, w_gu, w_down, layer, g, b, *, alpha):
    tm = 512
    idx, gates = moe_router(x, w_router, b_router, tm=512)
    src, slots, tables = moe_dispatch_tables(idx[:, :2], tm=tm)
    xs = gather_rows(x, src, rows=256)
    h = grouped_swiglu(xs, w_gu, layer, tables, tm=tm, tn=512)
    y = grouped_down(h, w_down, layer, tables, tm=tm, tn=512)
    return moe_combine_ln(y, slots, gates, x, g, b, rows=256, alpha=alpha)


def kernel(x, ev_w_in, ev_conv_w, ev_conv_b, ev_dt_bias, ev_a_log, ev_d_skip, ev_ssd_norm, ev_w_out, ev_ln1_g, ev_ln1_b, ev_ffn_w_gu, ev_ffn_w_down, ev_ln2_g, ev_ln2_b, od_w_qkv, od_w_out, od_ln1_g, od_ln1_b, od_router_w, od_router_b, od_exp_w_gu, od_exp_w_down, od_ln2_g, od_ln2_b):
    bsz, seq, d = x.shape
    t = bsz * seq
    depth = ev_w_in.shape[0] + od_w_qkv.shape[0]
    alpha = (2 * depth) ** 0.25
    qkvz = 3 * A_WIDTH + SSD_WIDTH + SSD_WIDTH + 2 * SSD_GROUPS * SSD_STATE

    xf = x.reshape(t, d).astype(F32)
    xb = xf.astype(BF16)
    for layer in range(depth):
        i = layer // 2
        if layer % 2 == 0:
            proj = matmul(xb, ev_w_in, i, n_out=qkvz, tm=1024, tn=1024, out_dtype=BF16)
            w_dt = jnp.pad(ev_w_in[i, :, qkvz:], ((0, 0), (0, LANES - SSD_HEADS)))[None]
            dt_raw = matmul(xb, w_dt, 0, n_out=LANES, tm=2048, tn=LANES, out_dtype=F32)
            proj = proj.reshape(bsz, seq, qkvz)
            y_a = dilated_attention(proj, bsz=bsz, seq=seq)
            y_b = ssd_mixer(proj, dt_raw.reshape(bsz, seq, LANES), ev_conv_w[i], ev_conv_b[i],
                            ev_dt_bias[i], ev_a_log[i], ev_d_skip[i], ev_ssd_norm[i],
                            bsz=bsz, seq=seq)
            mix = [y_a.reshape(t, A_WIDTH), y_b.reshape(t, SSD_WIDTH)]
            xf, xb = matmul_res_ln(mix, ev_w_out[i].astype(BF16), xf, ev_ln1_g[i], ev_ln1_b[i],
                                   tm=512, alpha=alpha)
            hid = matmul_swiglu(xb, ev_ffn_w_gu, i, tm=1024, tn=512)
            xf, xb = matmul_res_ln([hid], ev_ffn_w_down[i].astype(BF16), xf, ev_ln2_g[i], ev_ln2_b[i],
                                   tm=256, alpha=alpha)
        else:
            qkv = matmul(xb, od_w_qkv, i, n_out=3 * C_HEADS * HEAD_DIM, tm=1024, tn=1024, out_dtype=BF16)
            att = moba_attention(qkv.reshape(bsz, seq, 3 * C_HEADS * HEAD_DIM), bsz=bsz, seq=seq)
            xf, xb = matmul_res_ln([att.reshape(t, C_HEADS * HEAD_DIM)], od_w_out[i].astype(BF16), xf,
                                   od_ln1_g[i], od_ln1_b[i], tm=512, alpha=alpha)
            xf, xb = moe_layer(xf, od_router_w[i], od_router_b[i], od_exp_w_gu, od_exp_w_down, i,
                               od_ln2_g[i], od_ln2_b[i], alpha=alpha)
    return xf.reshape(bsz, seq, d).astype(x.dtype)
```

```python
import functools

import jax
import jax.numpy as jnp
from jax import lax
from jax.experimental import pallas as pl
from jax.experimental.pallas import tpu as pltpu

F32 = jnp.float32
BF16 = jnp.bfloat16

HEAD_DIM = 128
A_HEADS = 8
A_WIDTH = A_HEADS * HEAD_DIM
A_BRANCHES = ((128, 1), (512, 4), (2048, 16))
SSD_HEADS = 16
SSD_HEAD_DIM = 64
SSD_WIDTH = SSD_HEADS * SSD_HEAD_DIM
SSD_GROUPS = 4
SSD_STATE = 128
SSD_CONV = 4
SSD_CHUNK = 128
C_HEADS = 16
MOBA_BLOCK = 256
MOBA_TOPK = 3
N_EXPERTS = 8
LN_EPS = 1e-5
RMS_EPS = 1e-5
NEG = -1e30
LOG2E = 1.4426950408889634

LANES = 128
VMEM_LIMIT = 56 * 1024 * 1024


def _params(*sem):
    return pltpu.CompilerParams(dimension_semantics=sem, vmem_limit_bytes=VMEM_LIMIT)


def _silu(x):
    return x * (1.0 / (1.0 + jnp.exp(-x)))


def _layer_norm(z, g, b):
    mu = jnp.mean(z, axis=-1, keepdims=True)
    zc = z - mu
    var = jnp.mean(zc * zc, axis=-1, keepdims=True)
    return zc * lax.rsqrt(var + LN_EPS) * g + b


def _mm_kernel(a_ref, w_ref, o_ref, wb_ref, *, w_transposed):
    @pl.when(pl.program_id(1) == 0)
    def _():
        w = w_ref[...]
        wb_ref[...] = (w.T if w_transposed else w).astype(BF16)

    o_ref[...] = jnp.dot(a_ref[...], wb_ref[...], preferred_element_type=F32).astype(o_ref.dtype)


def matmul(a, w, layer, *, n_out, tm, tn, out_dtype, w_transposed=False):
    m, k = a.shape
    if w_transposed:
        w_spec = pl.BlockSpec((None, tn, k), lambda j, i: (layer, j, 0))
    else:
        w_spec = pl.BlockSpec((None, k, tn), lambda j, i: (layer, 0, j))
    return pl.pallas_call(
        functools.partial(_mm_kernel, w_transposed=w_transposed),
        out_shape=jax.ShapeDtypeStruct((m, n_out), out_dtype),
        grid=(n_out // tn, m // tm),
        in_specs=[pl.BlockSpec((tm, k), lambda j, i: (i, 0)), w_spec],
        out_specs=pl.BlockSpec((tm, tn), lambda j, i: (i, j)),
        scratch_shapes=[pltpu.VMEM((k, tn), BF16)],
        compiler_params=_params("parallel", "arbitrary"),
        name="matmul",
    )(a, w)


def _swiglu_kernel(a_ref, wg_ref, wu_ref, o_ref, wgb_ref, wub_ref):
    @pl.when(pl.program_id(1) == 0)
    def _():
        wgb_ref[...] = wg_ref[...].astype(BF16)
        wub_ref[...] = wu_ref[...].astype(BF16)

    a = a_ref[...]
    g = jnp.dot(a, wgb_ref[...], preferred_element_type=F32)
    u = jnp.dot(a, wub_ref[...], preferred_element_type=F32)
    o_ref[...] = (_silu(g) * u).astype(o_ref.dtype)


def matmul_swiglu(a, w_gu, layer, *, tm, tn):
    m, k = a.shape
    f = w_gu.shape[2] // 2
    nf = f // tn
    return pl.pallas_call(
        _swiglu_kernel,
        out_shape=jax.ShapeDtypeStruct((m, f), BF16),
        grid=(nf, m // tm),
        in_specs=[pl.BlockSpec((tm, k), lambda j, i: (i, 0)),
                  pl.BlockSpec((None, k, tn), lambda j, i: (layer, 0, j)),
                  pl.BlockSpec((None, k, tn), lambda j, i: (layer, 0, j + nf))],
        out_specs=pl.BlockSpec((tm, tn), lambda j, i: (i, j)),
        scratch_shapes=[pltpu.VMEM((k, tn), BF16)] * 2,
        compiler_params=_params("parallel", "arbitrary"),
        name="matmul_swiglu",
    )(a, w_gu, w_gu)


def _res_ln_kernel(*refs, widths, alpha):
    n = len(widths)
    a_refs, (w_ref, x_ref, g_ref, b_ref, o_ref, ob_ref) = refs[:n], refs[n:]
    z = alpha * x_ref[...]
    k0 = 0
    for a_ref, kw in zip(a_refs, widths):
        z = z + jnp.dot(a_ref[...], w_ref[k0:k0 + kw, :], preferred_element_type=F32)
        k0 += kw
    y = _layer_norm(z, g_ref[...], b_ref[...])
    o_ref[...] = y
    ob_ref[...] = y.astype(BF16)


def matmul_res_ln(a_parts, w, x, g, b, *, tm, alpha):
    m = x.shape[0]
    k, d = w.shape
    widths = tuple(a.shape[1] for a in a_parts)
    assert sum(widths) == k
    row = lambda width: pl.BlockSpec((tm, width), lambda i: (i, 0))
    const = lambda shape, **kw: pl.BlockSpec(shape, lambda i: (0, 0), **kw)
    return pl.pallas_call(
        functools.partial(_res_ln_kernel, widths=widths, alpha=alpha),
        out_shape=(jax.ShapeDtypeStruct((m, d), F32), jax.ShapeDtypeStruct((m, d), BF16)),
        grid=(m // tm,),
        in_specs=[row(width) for width in widths]
                 + [const((k, d), pipeline_mode=pl.Buffered(1)), row(d), const((1, d)), const((1, d))],
        out_specs=(row(d), row(d)),
        compiler_params=_params("parallel"),
        name="matmul_res_ln",
    )(*a_parts, w, x, g.reshape(1, d), b.reshape(1, d))


def _dilated_kernel(q_ref, k_ref, v_ref, o_ref, qf, kf, vf, qg, kg, vg, of, lf, mf, *, seq):
    (w0, d0), (w1, g), (w2, d2) = A_BRANCHES
    assert d0 == 1 and d2 == g * g and w0 // d0 == w1 // g == w2 // d2
    band = w0
    sub = seq // g
    assert seq // d2 == band
    scale = HEAD_DIM ** -0.5
    nt = (((1,), (1,)), ((), ()))

    qf[...] = q_ref[...].astype(F32)
    kf[...] = k_ref[...].astype(F32)
    vf[...] = v_ref[...].astype(F32)
    for r in range(g):
        nat = pl.ds(r, sub, stride=g)
        grp = pl.ds(r * sub, sub)
        qg[grp, :] = qf[nat, :]
        kg[grp, :] = kf[nat, :]
        vg[grp, :] = vf[nat, :]

    qi = lax.broadcasted_iota(jnp.int32, (band, 2 * band), 0)
    kj = lax.broadcasted_iota(jnp.int32, (band, 2 * band), 1)
    dist = band + qi - kj
    in_win2 = (dist >= 0) & (dist <= band)
    causal = (lax.broadcasted_iota(jnp.int32, (band, band), 1)
              <= lax.broadcasted_iota(jnp.int32, (band, band), 0))

    def attend(qb, kb, vb, mask):
        s = lax.dot_general(qb.astype(BF16), kb.astype(BF16), nt, preferred_element_type=F32) * scale
        s = jnp.where(mask, s, NEG)
        m = jnp.max(s, axis=-1, keepdims=True)
        p = jnp.exp(s - m)
        den = jnp.sum(p, axis=-1, keepdims=True)
        o = jnp.dot(p.astype(BF16), vb.astype(BF16), preferred_element_type=F32) / den
        return o, jnp.broadcast_to(m + jnp.log(den), (band, HEAD_DIM))

    def banded(bi, qs, ks, vs, n_blk):
        for idx in range(seq // band):
            rows = pl.ds(idx * band, band)
            if idx % n_blk == 0:
                o, lse = attend(qs[rows, :], ks[rows, :], vs[rows, :], causal)
            else:
                both = pl.ds((idx - 1) * band, 2 * band)
                o, lse = attend(qs[rows, :], ks[both, :], vs[both, :], in_win2)
            of[bi, rows, :] = o
            lf[bi, rows, :] = lse

    banded(0, q_ref, k_ref, v_ref, seq // band)
    banded(1, qg, kg, vg, sub // band)
    for r in range(g):
        for m in range(g):
            rows = pl.ds(r * sub + m, band, stride=g)
            o, lse = attend(qg[rows, :], kg[rows, :], vg[rows, :], causal)
            of[2, rows, :] = o
            lf[2, rows, :] = lse

    for r in range(g):
        nat = pl.ds(r, sub, stride=g)
        grp = pl.ds(r * sub, sub)
        l0, l1, l2 = lf[0, nat, :], lf[1, grp, :], lf[2, grp, :]
        mx = jnp.maximum(jnp.maximum(l0, l1), l2)
        e0, e1, e2 = jnp.exp(l0 - mx), jnp.exp(l1 - mx), jnp.exp(l2 - mx)
        mf[nat, :] = (e0 * of[0, nat, :] + e1 * of[1, grp, :] + e2 * of[2, grp, :]) / (e0 + e1 + e2)
    o_ref[...] = mf[...].astype(o_ref.dtype)


def dilated_attention(proj, *, bsz, seq):
    kern = functools.partial(_dilated_kernel, seq=seq)
    blk = lambda off: pl.BlockSpec((None, seq, HEAD_DIM), lambda b, h, off=off: (b, 0, h + off))
    return pl.pallas_call(
        kern,
        out_shape=jax.ShapeDtypeStruct((bsz, seq, A_WIDTH), BF16),
        grid=(bsz, A_HEADS),
        in_specs=[blk(0), blk(A_HEADS), blk(2 * A_HEADS)],
        out_specs=pl.BlockSpec((None, seq, HEAD_DIM), lambda b, h: (b, 0, h)),
        scratch_shapes=[pltpu.VMEM((seq, HEAD_DIM), F32)] * 6
                       + [pltpu.VMEM((len(A_BRANCHES), seq, HEAD_DIM), F32)] * 2
                       + [pltpu.VMEM((seq, HEAD_DIM), F32)],
        compiler_params=_params("parallel", "parallel"),
        name="dilated_attention",
    )(proj, proj, proj)


def _ssd_kernel(z_ref, xbc_ref, dt_ref, cw_ref, cb_ref, dtb_ref, alog_ref, dskip_ref, nw_ref,
                o_ref, tail, hstate):
    q = SSD_CHUNK
    c = pl.program_id(1)

    @pl.when(c == 0)
    def _():
        tail[:8, :] = jnp.zeros((8, tail.shape[1]), F32)
        hstate[...] = jnp.zeros_like(hstate)

    x_cur = xbc_ref[...].astype(F32)
    tail[8:, :] = x_cur
    conv = cb_ref[...]
    for j in range(SSD_CONV):
        off = 8 - (SSD_CONV - 1) + j
        conv = conv + cw_ref[j:j + 1, :] * tail[off:off + q, :]
    tail[:8, :] = x_cur[q - 8:, :]
    xbc = _silu(conv)
    gn = SSD_GROUPS * SSD_STATE
    xs = xbc[:, :SSD_WIDTH]
    bm = xbc[:, SSD_WIDTH:SSD_WIDTH + gn]
    cm = xbc[:, SSD_WIDTH + gn:]

    dt_in = dt_ref[...] + dtb_ref[...]
    dt = jnp.maximum(dt_in, 0.0) + jnp.log1p(jnp.exp(-jnp.abs(dt_in)))
    a_head = -jnp.exp(alog_ref[...])
    dta = dt * a_head
    li = lax.broadcasted_iota(jnp.int32, (q, q), 0)
    si = lax.broadcasted_iota(jnp.int32, (q, q), 1)
    causal = li >= si
    a_cs = jnp.dot(causal.astype(F32), dta, preferred_element_type=F32,
                   precision=lax.Precision.HIGHEST)
    a_cs_t = a_cs.T

    bm_b = bm.astype(BF16)
    cm_b = cm.astype(BF16)
    rep = SSD_HEADS // SSD_GROUPS
    ys = []
    for g in range(SSD_GROUPS):
        bg = bm_b[:, g * SSD_STATE:(g + 1) * SSD_STATE]
        cg = cm_b[:, g * SSD_STATE:(g + 1) * SSD_STATE]
        cbg = lax.dot_general(cg, bg, (((1,), (1,)), ((), ())), preferred_element_type=F32)
        bg_t = bm[:, g * SSD_STATE:(g + 1) * SSD_STATE].T
        for r in range(rep):
            h = g * rep + r
            col = a_cs[:, h:h + 1]
            row = a_cs_t[h:h + 1, :]
            decay_in = jnp.exp(jnp.where(causal, col - row, NEG))
            xh = xs[:, h * SSD_HEAD_DIM:(h + 1) * SSD_HEAD_DIM]
            xdt = (xh * dt[:, h:h + 1]).astype(BF16)
            y = jnp.dot((cbg * decay_in).astype(BF16), xdt, preferred_element_type=F32)
            h_prev = hstate[h]
            y_off = jnp.dot(cg, h_prev.astype(BF16), preferred_element_type=F32) * jnp.exp(col)
            last = a_cs_t[h:h + 1, q - 1:q]
            decay_st = jnp.exp(last - row)
            st = jnp.dot((bg_t * decay_st).astype(BF16), xdt, preferred_element_type=F32)
            hstate[h] = jnp.exp(last) * h_prev + st
            ys.append(y + y_off)
    y = jnp.concatenate(ys, axis=-1) + dskip_ref[...] * xs
    y = y * _silu(z_ref[...].astype(F32))
    gw = SSD_WIDTH // SSD_GROUPS
    outs = []
    for g in range(SSD_GROUPS):
        yg = y[:, g * gw:(g + 1) * gw]
        outs.append(yg * lax.rsqrt(jnp.mean(yg * yg, axis=-1, keepdims=True) + RMS_EPS))
    o_ref[...] = (jnp.concatenate(outs, axis=-1) * nw_ref[...]).astype(o_ref.dtype)


def ssd_mixer(proj, dt_raw, conv_w, conv_b, dt_bias, a_log, d_skip, norm_w, *, bsz, seq):
    cw = SSD_WIDTH + 2 * SSD_GROUPS * SSD_STATE
    pad = lambda v: jnp.pad(v.astype(F32), (0, LANES - SSD_HEADS)).reshape(1, LANES)
    dskip = jnp.repeat(d_skip.astype(F32), SSD_HEAD_DIM).reshape(1, SSD_WIDTH)
    const = lambda shape: pl.BlockSpec(shape, lambda b, c: (0, 0))
    z_blk = (3 * A_WIDTH) // SSD_WIDTH
    xbc_blk = (3 * A_WIDTH + SSD_WIDTH) // cw
    return pl.pallas_call(
        _ssd_kernel,
        out_shape=jax.ShapeDtypeStruct((bsz, seq, SSD_WIDTH), BF16),
        grid=(bsz, seq // SSD_CHUNK),
        in_specs=[pl.BlockSpec((None, SSD_CHUNK, SSD_WIDTH), lambda b, c: (b, c, z_blk)),
                  pl.BlockSpec((None, SSD_CHUNK, cw), lambda b, c: (b, c, xbc_blk)),
                  pl.BlockSpec((None, SSD_CHUNK, LANES), lambda b, c: (b, c, 0)),
                  const((SSD_CONV, cw)), const((1, cw)), const((1, LANES)), const((1, LANES)),
                  const((1, SSD_WIDTH)), const((1, SSD_WIDTH))],
        out_specs=pl.BlockSpec((None, SSD_CHUNK, SSD_WIDTH), lambda b, c: (b, c, 0)),
        scratch_shapes=[pltpu.VMEM((8 + SSD_CHUNK, cw), F32),
                        pltpu.VMEM((SSD_HEADS, SSD_STATE, SSD_HEAD_DIM), F32)],
        compiler_params=_params("parallel", "arbitrary"),
        name="ssd_mixer",
    )(proj, proj, dt_raw, conv_w.astype(F32), conv_b.astype(F32).reshape(1, cw), pad(dt_bias),
      pad(a_log), dskip, norm_w.astype(F32).reshape(1, SSD_WIDTH))


def _moba_kernel(q_ref, k_ref, v_ref, o_ref, kaug, qaug, *, seq):
    nb = seq // MOBA_BLOCK
    nbp = -(-nb // 8) * 8
    blk = MOBA_BLOCK
    n_sel = min(MOBA_TOPK, nb)
    nt = (((1,), (1,)), ((), ()))
    c_exp = (HEAD_DIM ** -0.5) * LOG2E

    kaug[:, :HEAD_DIM] = k_ref[...]
    row_blk = lax.broadcasted_iota(jnp.int32, (seq, LANES), 0) // blk
    col = lax.broadcasted_iota(jnp.int32, (seq, LANES), 1)
    kaug[:, HEAD_DIM:] = jnp.where(row_blk == col, 1.0, 0.0).astype(BF16)

    kmeans = [jnp.mean(k_ref[n * blk:(n + 1) * blk, :].astype(F32), axis=0, keepdims=True)
              for n in range(nb)]
    if nbp > nb:
        kmeans.append(jnp.zeros((nbp - nb, HEAD_DIM), F32))
    kmean = jnp.concatenate(kmeans, axis=0)

    sub = lax.broadcasted_iota(jnp.int32, (nbp, seq), 0)
    q_blk = lax.broadcasted_iota(jnp.int32, (nbp, seq), 1) // blk
    gate_t = lax.dot_general(kmean, q_ref[...].astype(F32), nt, preferred_element_type=F32,
                             precision=lax.Precision.HIGHEST)
    gate_t = jnp.where(sub < q_blk, gate_t, NEG)
    bias_t = jnp.zeros((nbp, seq), F32)
    for n in range(nb - 1):
        gn = gate_t[n:n + 1, :]
        beats = (gate_t > gn) | ((gate_t == gn) & (sub < n))
        rank = jnp.sum(beats.astype(F32), axis=0, keepdims=True)
        bias_t = jnp.where(sub == n, jnp.where(rank < n_sel, 0.0, NEG), bias_t)
    qaug[:, :HEAD_DIM] = q_ref[...]
    pad = jnp.zeros((LANES - nbp, blk), F32)
    for j in range(nb):
        bj = jnp.concatenate([bias_t[:, j * blk:(j + 1) * blk], pad], axis=0)
        qaug[j * blk:(j + 1) * blk, HEAD_DIM:] = bj.T.astype(BF16)

    qi = lax.broadcasted_iota(jnp.int32, (blk, blk), 0)
    kj = lax.broadcasted_iota(jnp.int32, (blk, blk), 1)
    causal = kj <= qi

    for j in range(nb):
        qb = q_ref[j * blk:(j + 1) * blk, :]
        s_self = lax.dot_general(qb, k_ref[j * blk:(j + 1) * blk, :], nt, preferred_element_type=F32)
        s_self = jnp.where(causal, s_self, NEG)
        m = jnp.max(s_self, axis=-1, keepdims=True)
        if j > 0:
            s_past = lax.dot_general(qaug[j * blk:(j + 1) * blk, :], kaug[:j * blk, :], nt,
                                     preferred_element_type=F32)
            m = jnp.maximum(m, jnp.max(s_past, axis=-1, keepdims=True))
        p = jnp.exp2((s_self - m) * c_exp)
        den = jnp.sum(p, axis=-1, keepdims=True)
        acc = jnp.dot(p.astype(BF16), v_ref[j * blk:(j + 1) * blk, :], preferred_element_type=F32)
        if j > 0:
            p = jnp.exp2((s_past - m) * c_exp)
            den = den + jnp.sum(p, axis=-1, keepdims=True)
            acc = acc + jnp.dot(p.astype(BF16), v_ref[:j * blk, :], preferred_element_type=F32)
        o_ref[j * blk:(j + 1) * blk, :] = (acc / den).astype(o_ref.dtype)


def moba_attention(qkv, *, bsz, seq):
    blk = lambda off: pl.BlockSpec((None, seq, HEAD_DIM), lambda b, h, off=off: (b, 0, h + off))
    return pl.pallas_call(
        functools.partial(_moba_kernel, seq=seq),
        out_shape=jax.ShapeDtypeStruct((bsz, seq, C_HEADS * HEAD_DIM), BF16),
        grid=(bsz, C_HEADS),
        in_specs=[blk(0), blk(C_HEADS), blk(2 * C_HEADS)],
        out_specs=pl.BlockSpec((None, seq, HEAD_DIM), lambda b, h: (b, 0, h)),
        scratch_shapes=[pltpu.VMEM((seq, 2 * HEAD_DIM), BF16)] * 2,
        compiler_params=_params("parallel", "parallel"),
        name="moba_attention",
    )(qkv, qkv, qkv)


def _router_kernel(x_ref, w_ref, b_ref, idx_ref, gate_ref):
    logits = jnp.dot(x_ref[...], w_ref[...], preferred_element_type=F32,
                     precision=lax.Precision.HIGHEST) + b_ref[...]
    tm = logits.shape[0]
    lane = lax.broadcasted_iota(jnp.int32, (tm, LANES), 1)
    logits = jnp.where(lane < N_EXPERTS, logits, NEG)
    v1 = jnp.max(logits, axis=-1, keepdims=True)
    i1 = jnp.min(jnp.where(logits == v1, lane, LANES), axis=-1, keepdims=True)
    rest = jnp.where(lane == i1, NEG, logits)
    v2 = jnp.max(rest, axis=-1, keepdims=True)
    i2 = jnp.min(jnp.where(rest == v2, lane, LANES), axis=-1, keepdims=True)
    e2 = jnp.exp(v2 - v1)
    g1 = 1.0 / (1.0 + e2)
    g2 = e2 / (1.0 + e2)
    idx_ref[...] = jnp.where(lane == 0, i1, jnp.where(lane == 1, i2, 0))
    gate_ref[...] = jnp.where(lane == 0, g1, jnp.where(lane == 1, g2, 0.0))


def moe_router(x, w_router, b_router, *, tm):
    t, d = x.shape
    w = jnp.pad(w_router.astype(F32), ((0, 0), (0, LANES - N_EXPERTS)))
    b = jnp.pad(b_router.astype(F32), (0, LANES - N_EXPERTS)).reshape(1, LANES)
    return pl.pallas_call(
        _router_kernel,
        out_shape=(jax.ShapeDtypeStruct((t, LANES), jnp.int32), jax.ShapeDtypeStruct((t, LANES), F32)),
        grid=(t // tm,),
        in_specs=[pl.BlockSpec((tm, d), lambda i: (i, 0)),
                  pl.BlockSpec((d, LANES), lambda i: (0, 0)),
                  pl.BlockSpec((1, LANES), lambda i: (0, 0))],
        out_specs=(pl.BlockSpec((tm, LANES), lambda i: (i, 0)),
                   pl.BlockSpec((tm, LANES), lambda i: (i, 0))),
        compiler_params=_params("parallel"),
        name="moe_router",
    )(x, w, b)


GATHER_AHEAD = 2


def _gather_kernel(head_ref, ahead_ref, x_hbm, o_ref, buf, sems, *, rows):
    i = pl.program_id(0)
    n_slot = GATHER_AHEAD + 1

    def issue(src_ref, first, slot):
        def body(r2, c):
            for k in range(2):
                r = 2 * r2 + k
                pltpu.make_async_copy(x_hbm.at[pl.ds(src_ref[0, 0, first + r], 1)],
                                      buf.at[slot, pl.ds(r, 1)], sems.at[slot]).start(priority=k)
            return c
        lax.fori_loop(0, rows // 2, body, 0, unroll=4)

    @pl.when(i == 0)
    def _():
        for blk in range(GATHER_AHEAD):
            issue(head_ref, blk * rows, blk)

    @pl.when(i + GATHER_AHEAD < pl.num_programs(0))
    def _():
        issue(ahead_ref, 0, (i + GATHER_AHEAD) % n_slot)

    slot = i % n_slot
    pltpu.make_async_copy(x_hbm.at[pl.ds(0, rows)], buf.at[slot], sems.at[slot]).wait()
    o_ref[...] = buf[slot].astype(o_ref.dtype)


def gather_rows(x, src, *, rows):
    n = src.shape[0]
    d = x.shape[1]
    nblk = n // rows
    assert nblk > GATHER_AHEAD
    src3 = src.reshape(nblk, 1, rows)
    head = src[:GATHER_AHEAD * rows].reshape(1, 1, GATHER_AHEAD * rows)
    return pl.pallas_call(
        functools.partial(_gather_kernel, rows=rows),
        out_shape=jax.ShapeDtypeStruct((n, d), BF16),
        grid=(nblk,),
        in_specs=[pl.BlockSpec((1, 1, GATHER_AHEAD * rows), lambda i: (0, 0, 0), memory_space=pltpu.SMEM),
                  pl.BlockSpec((1, 1, rows), lambda i: (jnp.minimum(i + GATHER_AHEAD, nblk - 1), 0, 0),
                               memory_space=pltpu.SMEM),
                  pl.BlockSpec(memory_space=pl.ANY)],
        out_specs=pl.BlockSpec((rows, d), lambda i: (i, 0)),
        scratch_shapes=[pltpu.VMEM((GATHER_AHEAD + 1, rows, d), F32),
                        pltpu.SemaphoreType.DMA((GATHER_AHEAD + 1,))],
        compiler_params=_params("arbitrary"),
        name="moe_gather",
    )(head, src3, x)


def _expert_changed(be_ref):
    i = pl.program_id(1)
    return (i == 0) | (be_ref[i] != be_ref[jnp.maximum(i - 1, 0)])


def _grouped_kernel(be_ref, nxt_ref, nu_ref, a_ref, w_hbm, o_ref, wf_ref, wb_ref, sems, seg_ref,
                    *, layer, tn, n_col, swiglu):
    j, i = pl.program_id(0), pl.program_id(1)
    halves = 2 if swiglu else 1

    def tile_copy(jj, e, slot, half):
        col = pl.multiple_of((jj + half * n_col) * tn, tn)
        return pltpu.make_async_copy(w_hbm.at[layer, e, :, pl.ds(col, tn)], wf_ref.at[slot, half],
                                     sems.at[slot, half])

    def fetch(jj, e, slot):
        for half in range(halves):
            tile_copy(jj, e, slot, half).start()

    @pl.when((j == 0) & (i == 0))
    def _():
        seg_ref[0] = 0
        fetch(0, be_ref[0], 0)

    @pl.when(_expert_changed(be_ref))
    def _():
        slot = seg_ref[0] % 2
        for half in range(halves):
            tile_copy(j, be_ref[i], slot, half).wait()
            wb_ref[half] = wf_ref[slot, half].astype(BF16)
        e_next = nxt_ref[i]

        @pl.when(e_next >= 0)
        def _():
            fetch(j, e_next, 1 - slot)

        @pl.when((e_next < 0) & (j + 1 < n_col))
        def _():
            fetch(j + 1, be_ref[0], 1 - slot)

        seg_ref[0] = seg_ref[0] + 1

    @pl.when(i < nu_ref[0])
    def _():
        a = a_ref[...]
        if swiglu:
            g = jnp.dot(a, wb_ref[0], preferred_element_type=F32)
            u = jnp.dot(a, wb_ref[1], preferred_element_type=F32)
            o_ref[...] = (_silu(g) * u).astype(o_ref.dtype)
        else:
            o_ref[...] = jnp.dot(a, wb_ref[0], preferred_element_type=F32).astype(o_ref.dtype)

    @pl.when(i >= nu_ref[0])
    def _():
        o_ref[...] = jnp.zeros_like(o_ref)


def _grouped_call(a, w, layer, tables, *, n_out, tm, tn, swiglu, out_dtype, name):
    block_expert, next_expert, n_used = tables
    n, k = a.shape
    n_col = n_out // tn
    halves = 2 if swiglu else 1
    grid_spec = pltpu.PrefetchScalarGridSpec(
        num_scalar_prefetch=3,
        grid=(n_col, n // tm),
        in_specs=[pl.BlockSpec((tm, k), lambda j, i, be, nx, nu: (i, 0)),
                  pl.BlockSpec(memory_space=pl.ANY)],
        out_specs=pl.BlockSpec((tm, tn), lambda j, i, be, nx, nu: (i, j)),
        scratch_shapes=[pltpu.VMEM((2, halves, k, tn), F32), pltpu.VMEM((halves, k, tn), BF16),
                        pltpu.SemaphoreType.DMA((2, halves)), pltpu.SMEM((1,), jnp.int32)],
    )
    return pl.pallas_call(
        functools.partial(_grouped_kernel, layer=layer, tn=tn, n_col=n_col, swiglu=swiglu),
        out_shape=jax.ShapeDtypeStruct((n, n_out), out_dtype),
        grid_spec=grid_spec,
        compiler_params=_params("arbitrary", "arbitrary"),
        name=name,
    )(block_expert, next_expert, n_used, a, w)


def grouped_swiglu(a, w_gu, layer, tables, *, tm, tn):
    return _grouped_call(a, w_gu, layer, tables, n_out=w_gu.shape[3] // 2, tm=tm, tn=tn, swiglu=True,
                         out_dtype=BF16, name="moe_grouped_swiglu")


def grouped_down(a, w_down, layer, tables, *, tm, tn):
    return _grouped_call(a, w_down, layer, tables, n_out=w_down.shape[3], tm=tm, tn=tn, swiglu=False,
                         out_dtype=F32, name="moe_grouped_down")


def _combine_kernel(slot0_ref, slot1_ref, y_hbm, gate_ref, x_ref, g_ref, b_ref, o_ref, ob_ref, buf, sems,
                    *, rows, alpha):
    i = pl.program_id(0)

    def issue(slot_ref, slot):
        def body(r, c):
            for k in range(2):
                pltpu.make_async_copy(y_hbm.at[pl.ds(slot_ref[0, k, r], 1)],
                                      buf.at[slot, pl.ds(k * rows + r, 1)], sems.at[slot]).start(priority=k)
            return c
        lax.fori_loop(0, rows, body, 0, unroll=4)

    @pl.when(i == 0)
    def _():
        issue(slot0_ref, 0)

    @pl.when(i + 1 < pl.num_programs(0))
    def _():
        issue(slot1_ref, (i + 1) % 2)

    slot = i % 2
    pltpu.make_async_copy(y_hbm.at[pl.ds(0, 2 * rows)], buf.at[slot], sems.at[slot]).wait()
    gates = gate_ref[...]
    h = gates[:, 0:1] * buf[slot, :rows, :] + gates[:, 1:2] * buf[slot, rows:, :]
    y = _layer_norm(alpha * x_ref[...] + h, g_ref[...], b_ref[...])
    o_ref[...] = y
    ob_ref[...] = y.astype(BF16)


def moe_combine_ln(y, slots, gates, x, g, b, *, rows, alpha):
    t, d = x.shape
    nblk = t // rows
    slots3 = slots.reshape(2, nblk, rows).transpose(1, 0, 2)
    return pl.pallas_call(
        functools.partial(_combine_kernel, rows=rows, alpha=alpha),
        out_shape=(jax.ShapeDtypeStruct((t, d), F32), jax.ShapeDtypeStruct((t, d), BF16)),
        grid=(nblk,),
        in_specs=[pl.BlockSpec((1, 2, rows), lambda i: (0, 0, 0), memory_space=pltpu.SMEM),
                  pl.BlockSpec((1, 2, rows), lambda i: (jnp.minimum(i + 1, nblk - 1), 0, 0),
                               memory_space=pltpu.SMEM),
                  pl.BlockSpec(memory_space=pl.ANY),
                  pl.BlockSpec((rows, LANES), lambda i: (i, 0)),
                  pl.BlockSpec((rows, d), lambda i: (i, 0)),
                  pl.BlockSpec((1, d), lambda i: (0, 0)),
                  pl.BlockSpec((1, d), lambda i: (0, 0))],
        out_specs=(pl.BlockSpec((rows, d), lambda i: (i, 0)),
                   pl.BlockSpec((rows, d), lambda i: (i, 0))),
        scratch_shapes=[pltpu.VMEM((2, 2 * rows, d), F32), pltpu.SemaphoreType.DMA((2,))],
        compiler_params=_params("arbitrary"),
        name="moe_combine_ln",
    )(slots3, slots3, y, gates, x, g.reshape(1, d), b.reshape(1, d))


def moe_dispatch_tables(top_idx, *, tm):
    t = top_idx.shape[0]
    n_rows = _padded_rows(t, tm)
    flat = top_idx.reshape(-1)
    onehot = (flat[:, None] == jnp.arange(N_EXPERTS)[None, :]).astype(jnp.int32)
    csum = jnp.cumsum(onehot, axis=0)
    rank = jnp.sum(csum * onehot, axis=1) - 1
    counts = csum[-1]
    padded = ((counts + tm - 1) // tm) * tm
    ends = jnp.cumsum(padded)
    starts = ends - padded
    dest = starts[flat] + rank
    src = jnp.zeros((n_rows,), jnp.int32).at[dest].set(jnp.arange(2 * t, dtype=jnp.int32) // 2)
    n_blk = n_rows // tm
    n_used = (ends[-1] // tm).astype(jnp.int32)
    blk_start = jnp.minimum(jnp.arange(n_blk, dtype=jnp.int32), n_used - 1) * tm
    block_expert = jnp.sum(blk_start[:, None] >= ends[None, :], axis=1).astype(jnp.int32)
    nxt_idx = jnp.searchsorted(block_expert, block_expert, side='right')
    next_expert = jnp.where(nxt_idx < n_blk, block_expert[jnp.minimum(nxt_idx, n_blk - 1)], -1)
    slots = dest.reshape(t, 2).T.astype(jnp.int32)
    return src, slots, (block_expert, next_expert.astype(jnp.int32), n_used.reshape(1))


def _padded_rows(t, tm):
    return 2 * t + N_EXPERTS * tm


def moe_layer(x, w_router, b_router, w_gu, w_down, layer, g, b, *, alpha):
    tm = 512
    idx, gates = moe_router(x, w_router, b_router, tm=512)
    src, slots, tables = moe_dispatch_tables(idx[:, :2], tm=tm)
    xs = gather_rows(x, src, rows=256)
    h = grouped_swiglu(xs, w_gu, layer, tables, tm=tm, tn=512)
    y = grouped_down(h, w_down, layer, tables, tm=tm, tn=512)
    return moe_combine_ln(y, slots, gates, x, g, b, rows=256, alpha=alpha)


def kernel(x, ev_w_in, ev_conv_w, ev_conv_b, ev_dt_bias, ev_a_log, ev_d_skip, ev_ssd_norm, ev_w_out, ev_ln1_g, ev_ln1_b, ev_ffn_w_gu, ev_ffn_w_down, ev_ln2_g, ev_ln2_b, od_w_qkv, od_w_out, od_ln1_g, od_ln1_b, od_router_w, od_router_b, od_exp_w_gu, od_exp_w_down, od_ln2_g, od_ln2_b):
    bsz, seq, d = x.shape
    t = bsz * seq
    depth = ev_w_in.shape[0] + od_w_qkv.shape[0]
    alpha = (2 * depth) ** 0.25
    qkvz = 3 * A_WIDTH + SSD_WIDTH + SSD_WIDTH + 2 * SSD_GROUPS * SSD_STATE

    w_in_t = jnp.swapaxes(ev_w_in, 1, 2)
    xf = x.reshape(t, d).astype(F32)
    xb = xf.astype(BF16)
    for layer in range(depth):
        i = layer // 2
        if layer % 2 == 0:
            proj = matmul(xb, w_in_t, i, n_out=qkvz, tm=1024, tn=1024, out_dtype=BF16, w_transposed=True)
            w_dt = jnp.pad(w_in_t[i, qkvz:, :], ((0, LANES - SSD_HEADS), (0, 0)))[None]
            dt_raw = matmul(xb, w_dt, 0, n_out=LANES, tm=2048, tn=LANES, out_dtype=F32, w_transposed=True)
            proj = proj.reshape(bsz, seq, qkvz)
            y_a = dilated_attention(proj, bsz=bsz, seq=seq)
            y_b = ssd_mixer(proj, dt_raw.reshape(bsz, seq, LANES), ev_conv_w[i], ev_conv_b[i],
                            ev_dt_bias[i], ev_a_log[i], ev_d_skip[i], ev_ssd_norm[i],
                            bsz=bsz, seq=seq)
            mix = [y_a.reshape(t, A_WIDTH), y_b.reshape(t, SSD_WIDTH)]
            xf, xb = matmul_res_ln(mix, ev_w_out[i].astype(BF16), xf, ev_ln1_g[i], ev_ln1_b[i],
                                   tm=512, alpha=alpha)
            hid = matmul_swiglu(xb, ev_ffn_w_gu, i, tm=1024, tn=512)
            xf, xb = matmul_res_ln([hid], ev_ffn_w_down[i].astype(BF16), xf, ev_ln2_g[i], ev_ln2_b[i],
                                   tm=256, alpha=alpha)
        else:
            qkv = matmul(xb, od_w_qkv, i, n_out=3 * C_HEADS * HEAD_DIM, tm=1024, tn=1024, out_dtype=BF16)
            att = moba_attention(qkv.reshape(bsz, seq, 3 * C_HEADS * HEAD_DIM), bsz=bsz, seq=seq)
            xf, xb = matmul_res_ln([att.reshape(t, C_HEADS * HEAD_DIM)], od_w_out[i].astype(BF16), xf,
                                   od_ln1_g[i], od_ln1_b[i], tm=512, alpha=alpha)
            xf, xb = moe_layer(xf, od_router_w[i], od_router_b[i], od_exp_w_gu, od_exp_w_down, i,
                               od_ln2_g[i], od_ln2_b[i], alpha=alpha)
    return xf.reshape(bsz, seq, d).astype(x.dtype)
```

```python
import functools

import jax
import jax.numpy as jnp
from jax import lax
from jax.experimental import pallas as pl
from jax.experimental.pallas import tpu as pltpu

F32 = jnp.float32
BF16 = jnp.bfloat16

HEAD_DIM = 128
A_HEADS = 8
A_WIDTH = A_HEADS * HEAD_DIM
A_BRANCHES = ((128, 1), (512, 4), (2048, 16))
SSD_HEADS = 16
SSD_HEAD_DIM = 64
SSD_WIDTH = SSD_HEADS * SSD_HEAD_DIM
SSD_GROUPS = 4
SSD_STATE = 128
SSD_CONV = 4
SSD_CHUNK = 128
C_HEADS = 16
MOBA_BLOCK = 256
MOBA_TOPK = 3
N_EXPERTS = 8
LN_EPS = 1e-5
RMS_EPS = 1e-5
NEG = -1e30
LOG2E = 1.4426950408889634

LANES = 128
VMEM_LIMIT = 56 * 1024 * 1024


def _params(*sem):
    return pltpu.CompilerParams(dimension_semantics=sem, vmem_limit_bytes=VMEM_LIMIT)


def _silu(x):
    return x * (1.0 / (1.0 + jnp.exp(-x)))


def _layer_norm(z, g, b):
    mu = jnp.mean(z, axis=-1, keepdims=True)
    zc = z - mu
    var = jnp.mean(zc * zc, axis=-1, keepdims=True)
    return zc * lax.rsqrt(var + LN_EPS) * g + b


def _mm_kernel(a_ref, w_ref, o_ref, wb_ref, *, w_transposed):
    @pl.when(pl.program_id(1) == 0)
    def _():
        w = w_ref[...]
        wb_ref[...] = (w.T if w_transposed else w).astype(BF16)

    o_ref[...] = jnp.dot(a_ref[...], wb_ref[...], preferred_element_type=F32).astype(o_ref.dtype)


def matmul(a, w, layer, *, n_out, tm, tn, out_dtype, w_transposed=False):
    m, k = a.shape
    if w_transposed:
        w_spec = pl.BlockSpec((None, tn, k), lambda j, i: (layer, j, 0))
    else:
        w_spec = pl.BlockSpec((None, k, tn), lambda j, i: (layer, 0, j))
    return pl.pallas_call(
        functools.partial(_mm_kernel, w_transposed=w_transposed),
        out_shape=jax.ShapeDtypeStruct((m, n_out), out_dtype),
        grid=(n_out // tn, m // tm),
        in_specs=[pl.BlockSpec((tm, k), lambda j, i: (i, 0)), w_spec],
        out_specs=pl.BlockSpec((tm, tn), lambda j, i: (i, j)),
        scratch_shapes=[pltpu.VMEM((k, tn), BF16)],
        compiler_params=_params("parallel", "arbitrary"),
        name="matmul",
    )(a, w)


def _swiglu_kernel(a_ref, wg_ref, wu_ref, o_ref, wgb_ref, wub_ref):
    @pl.when(pl.program_id(1) == 0)
    def _():
        wgb_ref[...] = wg_ref[...].astype(BF16)
        wub_ref[...] = wu_ref[...].astype(BF16)

    a = a_ref[...]
    g = jnp.dot(a, wgb_ref[...], preferred_element_type=F32)
    u = jnp.dot(a, wub_ref[...], preferred_element_type=F32)
    o_ref[...] = (_silu(g) * u).astype(o_ref.dtype)


def matmul_swiglu(a, w_gu, layer, *, tm, tn):
    m, k = a.shape
    f = w_gu.shape[2] // 2
    nf = f // tn
    return pl.pallas_call(
        _swiglu_kernel,
        out_shape=jax.ShapeDtypeStruct((m, f), BF16),
        grid=(nf, m // tm),
        in_specs=[pl.BlockSpec((tm, k), lambda j, i: (i, 0)),
                  pl.BlockSpec((None, k, tn), lambda j, i: (layer, 0, j)),
                  pl.BlockSpec((None, k, tn), lambda j, i: (layer, 0, j + nf))],
        out_specs=pl.BlockSpec((tm, tn), lambda j, i: (i, j)),
        scratch_shapes=[pltpu.VMEM((k, tn), BF16)] * 2,
        compiler_params=_params("parallel", "arbitrary"),
        name="matmul_swiglu",
    )(a, w_gu, w_gu)


def _res_ln_kernel(*refs, widths, alpha):
    n = len(widths)
    a_refs, (w_ref, x_ref, g_ref, b_ref, o_ref, ob_ref) = refs[:n], refs[n:]
    z = alpha * x_ref[...]
    k0 = 0
    for a_ref, kw in zip(a_refs, widths):
        z = z + jnp.dot(a_ref[...], w_ref[k0:k0 + kw, :], preferred_element_type=F32)
        k0 += kw
    y = _layer_norm(z, g_ref[...], b_ref[...])
    o_ref[...] = y
    ob_ref[...] = y.astype(BF16)


def matmul_res_ln(a_parts, w, x, g, b, *, tm, alpha):
    m = x.shape[0]
    k, d = w.shape
    widths = tuple(a.shape[1] for a in a_parts)
    assert sum(widths) == k
    row = lambda width: pl.BlockSpec((tm, width), lambda i: (i, 0))
    const = lambda shape, **kw: pl.BlockSpec(shape, lambda i: (0, 0), **kw)
    return pl.pallas_call(
        functools.partial(_res_ln_kernel, widths=widths, alpha=alpha),
        out_shape=(jax.ShapeDtypeStruct((m, d), F32), jax.ShapeDtypeStruct((m, d), BF16)),
        grid=(m // tm,),
        in_specs=[row(width) for width in widths]
                 + [const((k, d), pipeline_mode=pl.Buffered(1)), row(d), const((1, d)), const((1, d))],
        out_specs=(row(d), row(d)),
        compiler_params=_params("parallel"),
        name="matmul_res_ln",
    )(*a_parts, w, x, g.reshape(1, d), b.reshape(1, d))


def _dilated_kernel(q_ref, k_ref, v_ref, o_ref, qf, kf, vf, qg, kg, vg, of, lf, mf, *, seq):
    (w0, d0), (w1, g), (w2, d2) = A_BRANCHES
    assert d0 == 1 and d2 == g * g and w0 // d0 == w1 // g == w2 // d2
    band = w0
    sub = seq // g
    assert seq // d2 == band
    scale = HEAD_DIM ** -0.5
    nt = (((1,), (1,)), ((), ()))

    qf[...] = q_ref[...].astype(F32)
    kf[...] = k_ref[...].astype(F32)
    vf[...] = v_ref[...].astype(F32)
    for r in range(g):
        nat = pl.ds(r, sub, stride=g)
        grp = pl.ds(r * sub, sub)
        qg[grp, :] = qf[nat, :]
        kg[grp, :] = kf[nat, :]
        vg[grp, :] = vf[nat, :]

    qi = lax.broadcasted_iota(jnp.int32, (band, 2 * band), 0)
    kj = lax.broadcasted_iota(jnp.int32, (band, 2 * band), 1)
    dist = band + qi - kj
    in_win2 = (dist >= 0) & (dist <= band)
    causal = (lax.broadcasted_iota(jnp.int32, (band, band), 1)
              <= lax.broadcasted_iota(jnp.int32, (band, band), 0))

    def attend(qb, kb, vb, mask):
        s = lax.dot_general(qb.astype(BF16), kb.astype(BF16), nt, preferred_element_type=F32) * scale
        s = jnp.where(mask, s, NEG)
        m = jnp.max(s, axis=-1, keepdims=True)
        p = jnp.exp(s - m)
        den = jnp.sum(p, axis=-1, keepdims=True)
        o = jnp.dot(p.astype(BF16), vb.astype(BF16), preferred_element_type=F32) / den
        return o, jnp.broadcast_to(m + jnp.log(den), (band, HEAD_DIM))

    def banded(bi, qs, ks, vs, n_blk):
        for idx in range(seq // band):
            rows = pl.ds(idx * band, band)
            if idx % n_blk == 0:
                o, lse = attend(qs[rows, :], ks[rows, :], vs[rows, :], causal)
            else:
                both = pl.ds((idx - 1) * band, 2 * band)
                o, lse = attend(qs[rows, :], ks[both, :], vs[both, :], in_win2)
            of[bi, rows, :] = o
            lf[bi, rows, :] = lse

    banded(0, q_ref, k_ref, v_ref, seq // band)
    banded(1, qg, kg, vg, sub // band)
    for r in range(g):
        for m in range(g):
            rows = pl.ds(r * sub + m, band, stride=g)
            o, lse = attend(qg[rows, :], kg[rows, :], vg[rows, :], causal)
            of[2, rows, :] = o
            lf[2, rows, :] = lse

    for r in range(g):
        nat = pl.ds(r, sub, stride=g)
        grp = pl.ds(r * sub, sub)
        l0, l1, l2 = lf[0, nat, :], lf[1, grp, :], lf[2, grp, :]
        mx = jnp.maximum(jnp.maximum(l0, l1), l2)
        e0, e1, e2 = jnp.exp(l0 - mx), jnp.exp(l1 - mx), jnp.exp(l2 - mx)
        mf[nat, :] = (e0 * of[0, nat, :] + e1 * of[1, grp, :] + e2 * of[2, grp, :]) / (e0 + e1 + e2)
    o_ref[...] = mf[...].astype(o_ref.dtype)


def dilated_attention(proj, *, bsz, seq):
    kern = functools.partial(_dilated_kernel, seq=seq)
    blk = lambda off: pl.BlockSpec((None, seq, HEAD_DIM), lambda b, h, off=off: (b, 0, h + off))
    return pl.pallas_call(
        kern,
        out_shape=jax.ShapeDtypeStruct((bsz, seq, A_WIDTH), BF16),
        grid=(bsz, A_HEADS),
        in_specs=[blk(0), blk(A_HEADS), blk(2 * A_HEADS)],
        out_specs=pl.BlockSpec((None, seq, HEAD_DIM), lambda b, h: (b, 0, h)),
        scratch_shapes=[pltpu.VMEM((seq, HEAD_DIM), F32)] * 6
                       + [pltpu.VMEM((len(A_BRANCHES), seq, HEAD_DIM), F32)] * 2
                       + [pltpu.VMEM((seq, HEAD_DIM), F32)],
        compiler_params=_params("parallel", "parallel"),
        name="dilated_attention",
    )(proj, proj, proj)


def _ssd_kernel(z_ref, xbc_ref, dt_ref, cw_ref, cb_ref, dtb_ref, alog_ref, dskip_ref, nw_ref,
                o_ref, tail, hstate):
    q = SSD_CHUNK
    c = pl.program_id(1)

    @pl.when(c == 0)
    def _():
        tail[:8, :] = jnp.zeros((8, tail.shape[1]), F32)
        hstate[...] = jnp.zeros_like(hstate)

    x_cur = xbc_ref[...].astype(F32)
    tail[8:, :] = x_cur
    conv = cb_ref[...]
    for j in range(SSD_CONV):
        off = 8 - (SSD_CONV - 1) + j
        conv = conv + cw_ref[j:j + 1, :] * tail[off:off + q, :]
    tail[:8, :] = x_cur[q - 8:, :]
    xbc = _silu(conv)
    gn = SSD_GROUPS * SSD_STATE
    xs = xbc[:, :SSD_WIDTH]
    bm = xbc[:, SSD_WIDTH:SSD_WIDTH + gn]
    cm = xbc[:, SSD_WIDTH + gn:]

    dt_in = dt_ref[...] + dtb_ref[...]
    dt = jnp.maximum(dt_in, 0.0) + jnp.log1p(jnp.exp(-jnp.abs(dt_in)))
    a_head = -jnp.exp(alog_ref[...])
    dta = dt * a_head
    li = lax.broadcasted_iota(jnp.int32, (q, q), 0)
    si = lax.broadcasted_iota(jnp.int32, (q, q), 1)
    causal = li >= si
    a_cs = jnp.dot(causal.astype(F32), dta, preferred_element_type=F32,
                   precision=lax.Precision.HIGHEST)
    a_cs_t = a_cs.T

    bm_b = bm.astype(BF16)
    cm_b = cm.astype(BF16)
    rep = SSD_HEADS // SSD_GROUPS
    ys = []
    for g in range(SSD_GROUPS):
        bg = bm_b[:, g * SSD_STATE:(g + 1) * SSD_STATE]
        cg = cm_b[:, g * SSD_STATE:(g + 1) * SSD_STATE]
        cbg = lax.dot_general(cg, bg, (((1,), (1,)), ((), ())), preferred_element_type=F32)
        bg_t = bm[:, g * SSD_STATE:(g + 1) * SSD_STATE].T
        for r in range(rep):
            h = g * rep + r
            col = a_cs[:, h:h + 1]
            row = a_cs_t[h:h + 1, :]
            decay_in = jnp.exp(jnp.where(causal, col - row, NEG))
            xh = xs[:, h * SSD_HEAD_DIM:(h + 1) * SSD_HEAD_DIM]
            xdt = (xh * dt[:, h:h + 1]).astype(BF16)
            y = jnp.dot((cbg * decay_in).astype(BF16), xdt, preferred_element_type=F32)
            h_prev = hstate[h]
            y_off = jnp.dot(cg, h_prev.astype(BF16), preferred_element_type=F32) * jnp.exp(col)
            last = a_cs_t[h:h + 1, q - 1:q]
            decay_st = jnp.exp(last - row)
            st = jnp.dot((bg_t * decay_st).astype(BF16), xdt, preferred_element_type=F32)
            hstate[h] = jnp.exp(last) * h_prev + st
            ys.append(y + y_off)
    y = jnp.concatenate(ys, axis=-1) + dskip_ref[...] * xs
    y = y * _silu(z_ref[...].astype(F32))
    gw = SSD_WIDTH // SSD_GROUPS
    outs = []
    for g in range(SSD_GROUPS):
        yg = y[:, g * gw:(g + 1) * gw]
        outs.append(yg * lax.rsqrt(jnp.mean(yg * yg, axis=-1, keepdims=True) + RMS_EPS))
    o_ref[...] = (jnp.concatenate(outs, axis=-1) * nw_ref[...]).astype(o_ref.dtype)


def ssd_mixer(proj, dt_raw, conv_w, conv_b, dt_bias, a_log, d_skip, norm_w, *, bsz, seq):
    cw = SSD_WIDTH + 2 * SSD_GROUPS * SSD_STATE
    pad = lambda v: jnp.pad(v.astype(F32), (0, LANES - SSD_HEADS)).reshape(1, LANES)
    dskip = jnp.repeat(d_skip.astype(F32), SSD_HEAD_DIM).reshape(1, SSD_WIDTH)
    const = lambda shape: pl.BlockSpec(shape, lambda b, c: (0, 0))
    z_blk = (3 * A_WIDTH) // SSD_WIDTH
    xbc_blk = (3 * A_WIDTH + SSD_WIDTH) // cw
    return pl.pallas_call(
        _ssd_kernel,
        out_shape=jax.ShapeDtypeStruct((bsz, seq, SSD_WIDTH), BF16),
        grid=(bsz, seq // SSD_CHUNK),
        in_specs=[pl.BlockSpec((None, SSD_CHUNK, SSD_WIDTH), lambda b, c: (b, c, z_blk)),
                  pl.BlockSpec((None, SSD_CHUNK, cw), lambda b, c: (b, c, xbc_blk)),
                  pl.BlockSpec((None, SSD_CHUNK, LANES), lambda b, c: (b, c, 0)),
                  const((SSD_CONV, cw)), const((1, cw)), const((1, LANES)), const((1, LANES)),
                  const((1, SSD_WIDTH)), const((1, SSD_WIDTH))],
        out_specs=pl.BlockSpec((None, SSD_CHUNK, SSD_WIDTH), lambda b, c: (b, c, 0)),
        scratch_shapes=[pltpu.VMEM((8 + SSD_CHUNK, cw), F32),
                        pltpu.VMEM((SSD_HEADS, SSD_STATE, SSD_HEAD_DIM), F32)],
        compiler_params=_params("parallel", "arbitrary"),
        name="ssd_mixer",
    )(proj, proj, dt_raw, conv_w.astype(F32), conv_b.astype(F32).reshape(1, cw), pad(dt_bias),
      pad(a_log), dskip, norm_w.astype(F32).reshape(1, SSD_WIDTH))


def _moba_kernel(q_ref, k_ref, v_ref, o_ref, kaug, qaug, *, seq):
    nb = seq // MOBA_BLOCK
    nbp = -(-nb // 8) * 8
    blk = MOBA_BLOCK
    n_sel = min(MOBA_TOPK, nb)
    nt = (((1,), (1,)), ((), ()))
    c_exp = (HEAD_DIM ** -0.5) * LOG2E

    kaug[:, :HEAD_DIM] = k_ref[...]
    row_blk = lax.broadcasted_iota(jnp.int32, (seq, LANES), 0) // blk
    col = lax.broadcasted_iota(jnp.int32, (seq, LANES), 1)
    kaug[:, HEAD_DIM:] = jnp.where(row_blk == col, 1.0, 0.0).astype(BF16)

    kmeans = [jnp.mean(k_ref[n * blk:(n + 1) * blk, :].astype(F32), axis=0, keepdims=True)
              for n in range(nb)]
    if nbp > nb:
        kmeans.append(jnp.zeros((nbp - nb, HEAD_DIM), F32))
    kmean = jnp.concatenate(kmeans, axis=0)

    sub = lax.broadcasted_iota(jnp.int32, (nbp, seq), 0)
    q_blk = lax.broadcasted_iota(jnp.int32, (nbp, seq), 1) // blk
    gate_t = lax.dot_general(kmean, q_ref[...].astype(F32), nt, preferred_element_type=F32,
                             precision=lax.Precision.HIGHEST)
    gate_t = jnp.where(sub < q_blk, gate_t, NEG)
    bias_t = jnp.zeros((nbp, seq), F32)
    for n in range(nb - 1):
        gn = gate_t[n:n + 1, :]
        beats = (gate_t > gn) | ((gate_t == gn) & (sub < n))
        rank = jnp.sum(beats.astype(F32), axis=0, keepdims=True)
        bias_t = jnp.where(sub == n, jnp.where(rank < n_sel, 0.0, NEG), bias_t)
    qaug[:, :HEAD_DIM] = q_ref[...]
    pad = jnp.zeros((LANES - nbp, blk), F32)
    for j in range(nb):
        bj = jnp.concatenate([bias_t[:, j * blk:(j + 1) * blk], pad], axis=0)
        qaug[j * blk:(j + 1) * blk, HEAD_DIM:] = bj.T.astype(BF16)

    qi = lax.broadcasted_iota(jnp.int32, (blk, blk), 0)
    kj = lax.broadcasted_iota(jnp.int32, (blk, blk), 1)
    causal = kj <= qi

    for j in range(nb):
        qb = q_ref[j * blk:(j + 1) * blk, :]
        s_self = lax.dot_general(qb, k_ref[j * blk:(j + 1) * blk, :], nt, preferred_element_type=F32)
        s_self = jnp.where(causal, s_self, NEG)
        m = jnp.max(s_self, axis=-1, keepdims=True)
        if j > 0:
            s_past = lax.dot_general(qaug[j * blk:(j + 1) * blk, :], kaug[:j * blk, :], nt,
                                     preferred_element_type=F32)
            m = jnp.maximum(m, jnp.max(s_past, axis=-1, keepdims=True))
        p = jnp.exp2((s_self - m) * c_exp)
        den = jnp.sum(p, axis=-1, keepdims=True)
        acc = jnp.dot(p.astype(BF16), v_ref[j * blk:(j + 1) * blk, :], preferred_element_type=F32)
        if j > 0:
            p = jnp.exp2((s_past - m) * c_exp)
            den = den + jnp.sum(p, axis=-1, keepdims=True)
            acc = acc + jnp.dot(p.astype(BF16), v_ref[:j * blk, :], preferred_element_type=F32)
        o_ref[j * blk:(j + 1) * blk, :] = (acc / den).astype(o_ref.dtype)


def moba_attention(qkv, *, bsz, seq):
    blk = lambda off: pl.BlockSpec((None, seq, HEAD_DIM), lambda b, h, off=off: (b, 0, h + off))
    return pl.pallas_call(
        functools.partial(_moba_kernel, seq=seq),
        out_shape=jax.ShapeDtypeStruct((bsz, seq, C_HEADS * HEAD_DIM), BF16),
        grid=(bsz, C_HEADS),
        in_specs=[blk(0), blk(C_HEADS), blk(2 * C_HEADS)],
        out_specs=pl.BlockSpec((None, seq, HEAD_DIM), lambda b, h: (b, 0, h)),
        scratch_shapes=[pltpu.VMEM((seq, 2 * HEAD_DIM), BF16)] * 2,
        compiler_params=_params("parallel", "parallel"),
        name="moba_attention",
    )(qkv, qkv, qkv)


def _router_kernel(x_ref, w_ref, b_ref, idx_ref, gate_ref):
    logits = jnp.dot(x_ref[...], w_ref[...], preferred_element_type=F32,
                     precision=lax.Precision.HIGHEST) + b_ref[...]
    tm = logits.shape[0]
    lane = lax.broadcasted_iota(jnp.int32, (tm, LANES), 1)
    logits = jnp.where(lane < N_EXPERTS, logits, NEG)
    v1 = jnp.max(logits, axis=-1, keepdims=True)
    i1 = jnp.min(jnp.where(logits == v1, lane, LANES), axis=-1, keepdims=True)
    rest = jnp.where(lane == i1, NEG, logits)
    v2 = jnp.max(rest, axis=-1, keepdims=True)
    i2 = jnp.min(jnp.where(rest == v2, lane, LANES), axis=-1, keepdims=True)
    e2 = jnp.exp(v2 - v1)
    g1 = 1.0 / (1.0 + e2)
    g2 = e2 / (1.0 + e2)
    idx_ref[...] = jnp.where(lane == 0, i1, jnp.where(lane == 1, i2, 0))
    gate_ref[...] = jnp.where(lane == 0, g1, jnp.where(lane == 1, g2, 0.0))


def moe_router(x, w_router, b_router, *, tm):
    t, d = x.shape
    w = jnp.pad(w_router.astype(F32), ((0, 0), (0, LANES - N_EXPERTS)))
    b = jnp.pad(b_router.astype(F32), (0, LANES - N_EXPERTS)).reshape(1, LANES)
    return pl.pallas_call(
        _router_kernel,
        out_shape=(jax.ShapeDtypeStruct((t, LANES), jnp.int32), jax.ShapeDtypeStruct((t, LANES), F32)),
        grid=(t // tm,),
        in_specs=[pl.BlockSpec((tm, d), lambda i: (i, 0)),
                  pl.BlockSpec((d, LANES), lambda i: (0, 0)),
                  pl.BlockSpec((1, LANES), lambda i: (0, 0))],
        out_specs=(pl.BlockSpec((tm, LANES), lambda i: (i, 0)),
                   pl.BlockSpec((tm, LANES), lambda i: (i, 0))),
        compiler_params=_params("parallel"),
        name="moe_router",
    )(x, w, b)


GATHER_AHEAD = 2


def _gather_kernel(head_ref, ahead_ref, x_hbm, o_ref, buf, sems, *, rows):
    i = pl.program_id(0)
    n_slot = GATHER_AHEAD + 1

    def issue(src_ref, first, slot):
        def body(r2, c):
            for k in range(2):
                r = 2 * r2 + k
                pltpu.make_async_copy(x_hbm.at[pl.ds(src_ref[0, 0, first + r], 1)],
                                      buf.at[slot, pl.ds(r, 1)], sems.at[slot]).start(priority=k)
            return c
        lax.fori_loop(0, rows // 2, body, 0, unroll=4)

    @pl.when(i == 0)
    def _():
        for blk in range(GATHER_AHEAD):
            issue(head_ref, blk * rows, blk)

    @pl.when(i + GATHER_AHEAD < pl.num_programs(0))
    def _():
        issue(ahead_ref, 0, (i + GATHER_AHEAD) % n_slot)

    slot = i % n_slot
    pltpu.make_async_copy(x_hbm.at[pl.ds(0, rows)], buf.at[slot], sems.at[slot]).wait()
    o_ref[...] = buf[slot].astype(o_ref.dtype)


def gather_rows(x, src, *, rows):
    n = src.shape[0]
    d = x.shape[1]
    nblk = n // rows
    assert nblk > GATHER_AHEAD
    src3 = src.reshape(nblk, 1, rows)
    head = src[:GATHER_AHEAD * rows].reshape(1, 1, GATHER_AHEAD * rows)
    return pl.pallas_call(
        functools.partial(_gather_kernel, rows=rows),
        out_shape=jax.ShapeDtypeStruct((n, d), BF16),
        grid=(nblk,),
        in_specs=[pl.BlockSpec((1, 1, GATHER_AHEAD * rows), lambda i: (0, 0, 0), memory_space=pltpu.SMEM),
                  pl.BlockSpec((1, 1, rows), lambda i: (jnp.minimum(i + GATHER_AHEAD, nblk - 1), 0, 0),
                               memory_space=pltpu.SMEM),
                  pl.BlockSpec(memory_space=pl.ANY)],
        out_specs=pl.BlockSpec((rows, d), lambda i: (i, 0)),
        scratch_shapes=[pltpu.VMEM((GATHER_AHEAD + 1, rows, d), F32),
                        pltpu.SemaphoreType.DMA((GATHER_AHEAD + 1,))],
        compiler_params=_params("arbitrary"),
        name="moe_gather",
    )(head, src3, x)


def _expert_changed(be_ref):
    i = pl.program_id(1)
    return (i == 0) | (be_ref[i] != be_ref[jnp.maximum(i - 1, 0)])


def _grouped_kernel(be_ref, nxt_ref, nu_ref, a_ref, w_hbm, o_ref, wf_ref, wb_ref, sems, seg_ref,
                    *, layer, tn, n_col, swiglu):
    j, i = pl.program_id(0), pl.program_id(1)
    halves = 2 if swiglu else 1

    def tile_copy(jj, e, slot, half):
        col = pl.multiple_of((jj + half * n_col) * tn, tn)
        return pltpu.make_async_copy(w_hbm.at[layer, e, :, pl.ds(col, tn)], wf_ref.at[slot, half],
                                     sems.at[slot, half])

    def fetch(jj, e, slot):
        for half in range(halves):
            tile_copy(jj, e, slot, half).start()

    @pl.when((j == 0) & (i == 0))
    def _():
        seg_ref[0] = 0
        fetch(0, be_ref[0], 0)

    @pl.when(_expert_changed(be_ref))
    def _():
        slot = seg_ref[0] % 2
        for half in range(halves):
            tile_copy(j, be_ref[i], slot, half).wait()
            wb_ref[half] = wf_ref[slot, half].astype(BF16)
        e_next = nxt_ref[i]

        @pl.when(e_next >= 0)
        def _():
            fetch(j, e_next, 1 - slot)

        @pl.when((e_next < 0) & (j + 1 < n_col))
        def _():
            fetch(j + 1, be_ref[0], 1 - slot)

        seg_ref[0] = seg_ref[0] + 1

    @pl.when(i < nu_ref[0])
    def _():
        a = a_ref[...]
        if swiglu:
            g = jnp.dot(a, wb_ref[0], preferred_element_type=F32)
            u = jnp.dot(a, wb_ref[1], preferred_element_type=F32)
            o_ref[...] = (_silu(g) * u).astype(o_ref.dtype)
        else:
            o_ref[...] = jnp.dot(a, wb_ref[0], preferred_element_type=F32).astype(o_ref.dtype)

    @pl.when(i >= nu_ref[0])
    def _():
        o_ref[...] = jnp.zeros_like(o_ref)


def _grouped_call(a, w, layer, tables, *, n_out, tm, tn, swiglu, out_dtype, name):
    block_expert, next_expert, n_used = tables
    n, k = a.shape
    n_col = n_out // tn
    halves = 2 if swiglu else 1
    grid_spec = pltpu.PrefetchScalarGridSpec(
        num_scalar_prefetch=3,
        grid=(n_col, n // tm),
        in_specs=[pl.BlockSpec((tm, k), lambda j, i, be, nx, nu: (i, 0)),
                  pl.BlockSpec(memory_space=pl.ANY)],
        out_specs=pl.BlockSpec((tm, tn), lambda j, i, be, nx, nu: (i, j)),
        scratch_shapes=[pltpu.VMEM((2, halves, k, tn), F32), pltpu.VMEM((halves, k, tn), BF16),
                        pltpu.SemaphoreType.DMA((2, halves)), pltpu.SMEM((1,), jnp.int32)],
    )
    return pl.pallas_call(
        functools.partial(_grouped_kernel, layer=layer, tn=tn, n_col=n_col, swiglu=swiglu),
        out_shape=jax.ShapeDtypeStruct((n, n_out), out_dtype),
        grid_spec=grid_spec,
        compiler_params=_params("arbitrary", "arbitrary"),
        name=name,
    )(block_expert, next_expert, n_used, a, w)


def grouped_swiglu(a, w_gu, layer, tables, *, tm, tn):
    return _grouped_call(a, w_gu, layer, tables, n_out=w_gu.shape[3] // 2, tm=tm, tn=tn, swiglu=True,
                         out_dtype=BF16, name="moe_grouped_swiglu")


def grouped_down(a, w_down, layer, tables, *, tm, tn):
    return _grouped_call(a, w_down, layer, tables, n_out=w_down.shape[3], tm=tm, tn=tn, swiglu=False,
                         out_dtype=F32, name="moe_grouped_down")


def _combine_kernel(slot0_ref, slot1_ref, y_hbm, gate_ref, x_ref, g_ref, b_ref, o_ref, ob_ref, buf, sems,
                    *, rows, alpha):
    i = pl.program_id(0)

    def issue(slot_ref, slot):
        def body(r, c):
            for k in range(2):
                pltpu.make_async_copy(y_hbm.at[pl.ds(slot_ref[0, k, r], 1)],
                                      buf.at[slot, pl.ds(k * rows + r, 1)], sems.at[slot]).start(priority=k)
            return c
        lax.fori_loop(0, rows, body, 0, unroll=4)

    @pl.when(i == 0)
    def _():
        issue(slot0_ref, 0)

    @pl.when(i + 1 < pl.num_programs(0))
    def _():
        issue(slot1_ref, (i + 1) % 2)

    slot = i % 2
    pltpu.make_async_copy(y_hbm.at[pl.ds(0, 2 * rows)], buf.at[slot], sems.at[slot]).wait()
    gates = gate_ref[...]
    h = gates[:, 0:1] * buf[slot, :rows, :] + gates[:, 1:2] * buf[slot, rows:, :]
    y = _layer_norm(alpha * x_ref[...] + h, g_ref[...], b_ref[...])
    o_ref[...] = y
    ob_ref[...] = y.astype(BF16)


def moe_combine_ln(y, slots, gates, x, g, b, *, rows, alpha):
    t, d = x.shape
    nblk = t // rows
    slots3 = slots.reshape(2, nblk, rows).transpose(1, 0, 2)
    return pl.pallas_call(
        functools.partial(_combine_kernel, rows=rows, alpha=alpha),
        out_shape=(jax.ShapeDtypeStruct((t, d), F32), jax.ShapeDtypeStruct((t, d), BF16)),
        grid=(nblk,),
        in_specs=[pl.BlockSpec((1, 2, rows), lambda i: (0, 0, 0), memory_space=pltpu.SMEM),
                  pl.BlockSpec((1, 2, rows), lambda i: (jnp.minimum(i + 1, nblk - 1), 0, 0),
                               memory_space=pltpu.SMEM),
                  pl.BlockSpec(memory_space=pl.ANY),
                  pl.BlockSpec((rows, LANES), lambda i: (i, 0)),
                  pl.BlockSpec((rows, d), lambda i: (i, 0)),
                  pl.BlockSpec((1, d), lambda i: (0, 0)),
                  pl.BlockSpec((1, d), lambda i: (0, 0))],
        out_specs=(pl.BlockSpec((rows, d), lambda i: (i, 0)),
                   pl.BlockSpec((rows, d), lambda i: (i, 0))),
        scratch_shapes=[pltpu.VMEM((2, 2 * rows, d), F32), pltpu.SemaphoreType.DMA((2,))],
        compiler_params=_params("arbitrary"),
        name="moe_combine_ln",
    )(slots3, slots3, y, gates, x, g.reshape(1, d), b.reshape(1, d))


def moe_dispatch_tables(top_idx, *, tm):
    t = top_idx.shape[0]
    n_rows = _padded_rows(t, tm)
    flat = top_idx.reshape(-1)
    onehot = (flat[:, None] == jnp.arange(N_EXPERTS)[None, :]).astype(jnp.int32)
    csum = jnp.cumsum(onehot, axis=0)
    rank = jnp.sum(csum * onehot, axis=1) - 1
    counts = csum[-1]
    padded = ((counts + tm - 1) // tm) * tm
    ends = jnp.cumsum(padded)
    starts = ends - padded
    dest = starts[flat] + rank
    src = jnp.zeros((n_rows,), jnp.int32).at[dest].set(jnp.arange(2 * t, dtype=jnp.int32) // 2)
    n_blk = n_rows // tm
    n_used = (ends[-1] // tm).astype(jnp.int32)
    blk_start = jnp.minimum(jnp.arange(n_blk, dtype=jnp.int32), n_used - 1) * tm
    block_expert = jnp.sum(blk_start[:, None] >= ends[None, :], axis=1).astype(jnp.int32)
    nxt_idx = jnp.searchsorted(block_expert, block_expert, side='right')
    next_expert = jnp.where(nxt_idx < n_blk, block_expert[jnp.minimum(nxt_idx, n_blk - 1)], -1)
    slots = dest.reshape(t, 2).T.astype(jnp.int32)
    return src, slots, (block_expert, next_expert.astype(jnp.int32), n_used.reshape(1))


def _padded_rows(t, tm):
    return 2 * t + N_EXPERTS * tm


def moe_layer(x, w_router, b_router, w_gu, w_down, layer, g, b, *, alpha):
    tm = 512
    idx, gates = moe_router(x, w_router, b_router, tm=512)
    src, slots, tables = moe_dispatch_tables(idx[:, :2], tm=tm)
    xs = gather_rows(x, src, rows=256)
    h = grouped_swiglu(xs, w_gu, layer, tables, tm=tm, tn=512)
    y = grouped_down(h, w_down, layer, tables, tm=tm, tn=512)
    return moe_combine_ln(y, slots, gates, x, g, b, rows=256, alpha=alpha)


def kernel(x, ev_w_in, ev_conv_w, ev_conv_b, ev_dt_bias, ev_a_log, ev_d_skip, ev_ssd_norm, ev_w_out, ev_ln1_g, ev_ln1_b, ev_ffn_w_gu, ev_ffn_w_down, ev_ln2_g, ev_ln2_b, od_w_qkv, od_w_out, od_ln1_g, od_ln1_b, od_router_w, od_router_b, od_exp_w_gu, od_exp_w_down, od_ln2_g, od_ln2_b):
    bsz, seq, d = x.shape
    t = bsz * seq
    depth = ev_w_in.shape[0] + od_w_qkv.shape[0]
    alpha = (2 * depth) ** 0.25
    qkvz = 3 * A_WIDTH + SSD_WIDTH + SSD_WIDTH + 2 * SSD_GROUPS * SSD_STATE

    w_in_t = jnp.swapaxes(ev_w_in, 1, 2)
    xf = x.reshape(t, d).astype(F32)
    xb = xf.astype(BF16)
    for layer in range(depth):
        i = layer // 2
        if layer % 2 == 0:
            proj = matmul(xb, w_in_t, i, n_out=qkvz, tm=2048, tn=1024, out_dtype=BF16, w_transposed=True)
            w_dt = jnp.pad(w_in_t[i, qkvz:, :], ((0, LANES - SSD_HEADS), (0, 0)))[None]
            dt_raw = matmul(xb, w_dt, 0, n_out=LANES, tm=2048, tn=LANES, out_dtype=F32, w_transposed=True)
            proj = proj.reshape(bsz, seq, qkvz)
            y_a = dilated_attention(proj, bsz=bsz, seq=seq)
            y_b = ssd_mixer(proj, dt_raw.reshape(bsz, seq, LANES), ev_conv_w[i], ev_conv_b[i],
                            ev_dt_bias[i], ev_a_log[i], ev_d_skip[i], ev_ssd_norm[i],
                            bsz=bsz, seq=seq)
            mix = [y_a.reshape(t, A_WIDTH), y_b.reshape(t, SSD_WIDTH)]
            xf, xb = matmul_res_ln(mix, ev_w_out[i].astype(BF16), xf, ev_ln1_g[i], ev_ln1_b[i],
                                   tm=512, alpha=alpha)
            hid = matmul_swiglu(xb, ev_ffn_w_gu, i, tm=2048, tn=512)
            xf, xb = matmul_res_ln([hid], ev_ffn_w_down[i].astype(BF16), xf, ev_ln2_g[i], ev_ln2_b[i],
                                   tm=256, alpha=alpha)
        else:
            qkv = matmul(xb, od_w_qkv, i, n_out=3 * C_HEADS * HEAD_DIM, tm=2048, tn=1024, out_dtype=BF16)
            att = moba_attention(qkv.reshape(bsz, seq, 3 * C_HEADS * HEAD_DIM), bsz=bsz, seq=seq)
            xf, xb = matmul_res_ln([att.reshape(t, C_HEADS * HEAD_DIM)], od_w_out[i].astype(BF16), xf,
                                   od_ln1_g[i], od_ln1_b[i], tm=512, alpha=alpha)
            xf, xb = moe_layer(xf, od_router_w[i], od_router_b[i], od_exp_w_gu, od_exp_w_down, i,
                               od_ln2_g[i], od_ln2_b[i], alpha=alpha)
    return xf.reshape(bsz, seq, d).astype(x.dtype)
```
